```python
import math
import jax, jax.numpy as jnp
from jax import lax
import numpy as np

D_MODEL = 1024
BATCH = 4
SEQ = 8192
DEPTH = 1

CTX_LEN = 256
GRID_W = 64
CHUNK = 64
EPS = 1e-6
A_HEADS = 8
A_DK = 128
A_DV = 128
A_CONV = 3
B_HEADS = 8
B_DK = 128
B_DV = 128
P_HEADS = 8
P_NKEYS = 128
P_EXPERTS = P_NKEYS * P_NKEYS
P_TOPK = 16
P_DQ = 128
P_BLOCK = 128

A_QKV = A_HEADS * (2 * A_DK + A_DV)
A_GATE = A_HEADS * A_DV
A_AB = 4 * A_HEADS
B_QK = B_HEADS * B_DK
B_V = B_HEADS * B_DV
MERGE_G = 2 * D_MODEL
SPLIT_IDX = (A_QKV,
             A_QKV + A_GATE,
             A_QKV + A_GATE + A_AB,
             A_QKV + A_GATE + A_AB + B_QK,
             A_QKV + A_GATE + A_AB + 3 * B_QK,
             A_QKV + A_GATE + A_AB + 3 * B_QK + B_V,
             A_QKV + A_GATE + A_AB + 3 * B_QK + 2 * B_V)
N_IN = A_QKV + A_GATE + A_AB + 3 * B_QK + 2 * B_V + MERGE_G

kernel_name = 'hybrid_gdn_hgrn2_peer_prefix_dit'


def _rmsnorm(x, w):
    xf = x.astype(jnp.float32)
    y = xf * lax.rsqrt(jnp.mean(xf * xf, axis=-1, keepdims=True) + EPS)
    return (y * w.astype(jnp.float32)).astype(x.dtype)


def _l2norm(x):
    return x * lax.rsqrt(jnp.sum(x * x, axis=-1, keepdims=True) + EPS)


def _modulate(xn, shift, scale):
    return xn * (1 + scale) + shift


def _heads(x, n_heads):
    b, l, _ = x.shape
    return x.reshape(b, l, n_heads, -1).transpose(0, 2, 1, 3)


def _unheads(x):
    b, h, l, d = x.shape
    return x.transpose(0, 2, 1, 3).reshape(b, l, h * d)


def _flip(t):
    return jnp.flip(t, axis=2)


def _short_conv(x, w):
    pad = w.shape[0] // 2
    return lax.conv_general_dilated(x, w[:, None, :].astype(x.dtype), window_strides=(1,),
                                    padding=[(pad, pad)], dimension_numbers=('NWC', 'WIO', 'NWC'),
                                    feature_group_count=x.shape[-1])


def _gdn_chunked(q, k, v, g, beta, s0):
    b, h, L, _ = q.shape
    dv = v.shape[-1]
    n = L // CHUNK
    q, k, v = (t.reshape(b, h, n, CHUNK, -1) for t in (q, k, v))
    g, beta = (t.reshape(b, h, n, CHUNK) for t in (g, beta))
    gc = jnp.cumsum(g, axis=-1)
    incl = jnp.tril(jnp.ones((CHUNK, CHUNK), bool))
    strict = jnp.tril(jnp.ones((CHUNK, CHUNK), bool), -1)
    diff = gc[..., :, None] - gc[..., None, :]
    decay = jnp.where(incl, jnp.exp(jnp.where(incl, diff, 0.0)), 0.0)
    kb = k * beta[..., None]
    m = jnp.where(strict, jnp.einsum('bhnid,bhnjd->bhnij', kb, k) * decay, 0.0)
    a = m + jnp.eye(CHUNK, dtype=m.dtype)
    rhs = jnp.concatenate([v * beta[..., None], kb * jnp.exp(gc)[..., None]], axis=-1)
    sol = lax.linalg.triangular_solve(a, rhs, left_side=True, lower=True, unit_diagonal=True)
    u, w = sol[..., :dv], sol[..., dv:]
    qk = jnp.einsum('bhnid,bhnjd->bhnij', q, k) * decay
    q_dec = q * jnp.exp(gc)[..., None]
    k_dec = k * jnp.exp(gc[..., -1:] - gc)[..., None]
    tot = jnp.exp(gc[..., -1])

    def step(S, xs):
        u_c, w_c, qd_c, kd_c, qk_c, tot_c = xs
        v_new = u_c - jnp.einsum('bhcd,bhdv->bhcv', w_c, S)
        o_c = jnp.einsum('bhcd,bhdv->bhcv', qd_c, S) + jnp.einsum('bhij,bhjv->bhiv', qk_c, v_new)
        S = tot_c[..., None, None] * S + jnp.einsum('bhcd,bhcv->bhdv', kd_c, v_new)
        return S, o_c

    xs = tuple(jnp.moveaxis(t, 2, 0) for t in (u, w, q_dec, k_dec, qk, tot))
    s_fin, o = lax.scan(step, s0, xs)
    return jnp.moveaxis(o, 0, 2).reshape(b, h, L, dv), s_fin


def _gla_chunked(q, k, v, logf, s0):
    b, h, L, _ = q.shape
    dv = v.shape[-1]
    n = L // CHUNK
    q, k, v, logf = (t.reshape(b, h, n, CHUNK, -1) for t in (q, k, v, logf))
    bc = jnp.cumsum(logf, axis=3)
    ref = bc[..., CHUNK // 2:CHUNK // 2 + 1, :]
    incl = jnp.tril(jnp.ones((CHUNK, CHUNK), bool))
    scores = jnp.einsum('bhnid,bhnjd->bhnij', q * jnp.exp(bc - ref), k * jnp.exp(ref - bc))
    scores = jnp.where(incl, scores, 0.0)
    o_intra = jnp.einsum('bhnij,bhnjv->bhniv', scores, v)
    q_dec = q * jnp.exp(bc)
    k_dec = k * jnp.exp(bc[..., -1:, :] - bc)
    tot = jnp.exp(bc[..., -1, :])

    def step(S, xs):
        qd_c, kd_c, v_c, tot_c = xs
        o_c = jnp.einsum('bhcd,bhdv->bhcv', qd_c, S)
        S = tot_c[..., :, None] * S + jnp.einsum('bhcd,bhcv->bhdv', kd_c, v_c)
        return S, o_c

    xs = tuple(jnp.moveaxis(t, 2, 0) for t in (q_dec, k_dec, v, tot))
    s_fin, o_inter = lax.scan(step, s0, xs)
    o = o_intra + jnp.moveaxis(o_inter, 0, 2)
    return o.reshape(b, h, L, dv), s_fin


def _gdn_branch(qkv, gate, ab, conv_w, a_log, dt_bias, norm_w, s0):
    f32 = jnp.float32
    dtype = qkv.dtype
    b, L, _ = qkv.shape
    qkv = jax.nn.silu(_short_conv(qkv, conv_w)).astype(f32)
    q, k, v = jnp.split(qkv, [A_HEADS * A_DK, 2 * A_HEADS * A_DK], axis=-1)
    q = _l2norm(_heads(q, A_HEADS)) * (A_DK ** -0.5)
    k = _l2norm(_heads(k, A_HEADS))
    v = _heads(v, A_HEADS)
    ab = ab.astype(f32).reshape(b, L, 4, A_HEADS).transpose(0, 2, 3, 1)
    g = -jnp.exp(a_log.astype(f32))[None, :, :, None] * jax.nn.softplus(
        ab[:, :2] + dt_bias.astype(f32)[None, :, :, None])
    beta = jax.nn.sigmoid(ab[:, 2:])
    o_f, s_f = _gdn_chunked(q, k, v, g[:, 0], beta[:, 0], s0[0])
    o_b, s_b = _gdn_chunked(_flip(q), _flip(k), _flip(v), _flip(g[:, 1]), _flip(beta[:, 1]), s0[1])
    o = o_f + _flip(o_b)
    o = _rmsnorm(o, norm_w) * jax.nn.silu(_heads(gate.astype(f32), A_HEADS))
    return _unheads(o).astype(dtype), (s_f, s_b)


def _hgrn2_branch(q, f2, i, gate, lb, norm_w, s0):
    f32 = jnp.float32
    dtype = q.dtype
    q = _heads(jax.nn.silu(q.astype(f32)), B_HEADS) * (B_DK ** -0.5)
    v = _heads(i.astype(f32), B_HEADS)
    lb = lb.astype(f32)

    def forget(fx):
        f = lb + (1 - lb) * jax.nn.sigmoid(fx)
        return _heads(jnp.log(f), B_HEADS), _heads(1 - f, B_HEADS)

    ff, fb = jnp.split(f2.astype(f32), 2, axis=-1)
    logf_f, k_f = forget(ff)
    logf_b, k_b = forget(fb)
    o_f, s_f = _gla_chunked(q, k_f, v, logf_f, s0[0])
    o_b, s_b = _gla_chunked(_flip(q), _flip(k_b), _flip(v), _flip(logf_b), s0[1])
    o = o_f + _flip(o_b)
    o = _rmsnorm(o, norm_w) * jax.nn.silu(_heads(gate.astype(f32), B_HEADS))
    return _unheads(o).astype(dtype), (s_f, s_b)


def _token_mixers(h, hc, need_ctx, w_in, conv_w, a_log, dt_bias, gdn_norm_w, lb, hg_norm_w, w_pa, w_pb, w_o):
    b, s, _ = h.shape
    rows = s // GRID_W
    qkv, ga, ab, qb, fb2, ib, gb, mg = jnp.split(h @ w_in, SPLIT_IDX, axis=-1)
    qkv_c, ga_c, ab_c, qb_c, fb2_c, ib_c, gb_c, mg_c = jnp.split(hc @ w_in, SPLIT_IDX, axis=-1)
    za = jnp.zeros((b, A_HEADS, A_DK, A_DV), jnp.float32)
    zb = jnp.zeros((b, B_HEADS, B_DK, B_DV), jnp.float32)
    oa_c, sa = _gdn_branch(qkv_c, ga_c, ab_c, conv_w, a_log, dt_bias, gdn_norm_w, (za, za))
    oa, _ = _gdn_branch(qkv, ga, ab, conv_w, a_log, dt_bias, gdn_norm_w, sa)
    to_col = lambda t: t.reshape(b, rows, GRID_W, -1).transpose(0, 2, 1, 3).reshape(b, s, -1)
    from_col = lambda t: t.reshape(b, GRID_W, rows, -1).transpose(0, 2, 1, 3).reshape(b, s, -1)
    ob_c, sb = _hgrn2_branch(qb_c, fb2_c, ib_c, gb_c, lb, hg_norm_w, (zb, zb))
    ob, _ = _hgrn2_branch(to_col(qb), to_col(fb2), to_col(ib), to_col(gb), lb, hg_norm_w, sb)
    ob = from_col(ob)

    def merge(o_a, o_b, gates):
        g_a, g_b = jnp.split(jax.nn.sigmoid(gates), 2, axis=-1)
        return (g_a * (o_a @ w_pa) + g_b * (o_b @ w_pb)) @ w_o

    mix = merge(oa, ob, mg)
    mix_c = merge(oa_c, ob_c, mg_c) if need_ctx else None
    return mix, mix_c


def _peer(h, w_query, sub_keys, expert_u, expert_v):
    b, s, d = h.shape
    n_tok = b * s
    t = h.reshape(n_tok, d)
    q = (t @ w_query).reshape(n_tok, P_HEADS, 2, P_DQ)
    sc = jnp.einsum('thcd,hckd->thck', q, sub_keys)
    s1, i1 = lax.top_k(sc[:, :, 0], P_TOPK)
    s2, i2 = lax.top_k(sc[:, :, 1], P_TOPK)
    cand = (s1[..., :, None] + s2[..., None, :]).reshape(n_tok, P_HEADS, P_TOPK * P_TOPK)
    top_s, top_i = lax.top_k(cand, P_TOPK)
    e1 = jnp.take_along_axis(i1, top_i // P_TOPK, axis=-1)
    e2 = jnp.take_along_axis(i2, top_i % P_TOPK, axis=-1)
    nb = n_tok // P_BLOCK
    experts = (e1 * P_NKEYS + e2).reshape(nb, P_BLOCK, P_HEADS * P_TOPK)
    gates = jax.nn.softmax(top_s.astype(jnp.float32), axis=-1).astype(h.dtype)
    gates = gates.reshape(nb, P_BLOCK, P_HEADS * P_TOPK)

    def block(args):
        tb, eb, gb = args
        act = jax.nn.gelu(jnp.einsum('td,tkd->tk', tb, expert_u[eb]), approximate=False) * gb
        return jnp.einsum('tk,tkd->td', act, expert_v[eb])

    y = lax.map(block, (t.reshape(nb, P_BLOCK, d), experts, gates))
    return y.reshape(b, s, d)


def setup_inputs(seed: int = 0) -> dict:
    key = jax.random.key(seed)
    ks = jax.random.split(key, 24)
    f32 = jnp.float32
    nrm = lambda k, shape, scale: jax.random.normal(k, shape, f32) * scale
    dt = jnp.exp(jax.random.uniform(ks[10], (DEPTH, 2, A_HEADS), f32, math.log(1e-3), math.log(1e-1)))
    return {
        'x': nrm(ks[0], (BATCH, SEQ, D_MODEL), 1.0),
        'c': nrm(ks[1], (BATCH, D_MODEL), 1.0),
        'ctx': nrm(ks[2], (BATCH, CTX_LEN, D_MODEL), 1.0),
        'c_ctx': nrm(ks[3], (D_MODEL,), 1.0),
        'w_ada': nrm(ks[4], (DEPTH, D_MODEL, 6 * D_MODEL), 0.5 * D_MODEL ** -0.5),
        'b_ada': nrm(ks[5], (DEPTH, 6 * D_MODEL), 0.02),
        'norm1_w': 1.0 + nrm(ks[6], (DEPTH, D_MODEL), 0.02),
        'w_in': nrm(ks[7], (DEPTH, D_MODEL, N_IN), D_MODEL ** -0.5),
        'conv_w': nrm(ks[8], (DEPTH, A_CONV, A_QKV), A_CONV ** -0.5),
        'a_log': jnp.log(jax.random.uniform(ks[9], (DEPTH, 2, A_HEADS), f32, 1.0, 16.0)),
        'dt_bias': dt + jnp.log(-jnp.expm1(-dt)),
        'gdn_norm_w': 1.0 + nrm(ks[11], (DEPTH, A_DV), 0.02),
        'lb_logits': nrm(ks[12], (DEPTH + 1, B_HEADS * B_DK), 0.1),
        'hg_norm_w': 1.0 + nrm(ks[13], (DEPTH, B_DV), 0.02),
        'w_pa': nrm(ks[14], (DEPTH, A_HEADS * A_DV, D_MODEL), (A_HEADS * A_DV) ** -0.5),
        'w_pb': nrm(ks[15], (DEPTH, B_HEADS * B_DV, D_MODEL), (B_HEADS * B_DV) ** -0.5),
        'w_o': nrm(ks[16], (DEPTH, D_MODEL, D_MODEL), D_MODEL ** -0.5),
        'norm2_w': 1.0 + nrm(ks[17], (DEPTH, D_MODEL), 0.02),
        'w_query': nrm(ks[18], (DEPTH, D_MODEL, P_HEADS * 2 * P_DQ), D_MODEL ** -0.5),
        'sub_keys': nrm(ks[19], (DEPTH, P_HEADS, 2, P_NKEYS, P_DQ), P_DQ ** -0.5),
        'expert_u': nrm(ks[20], (DEPTH, P_EXPERTS, D_MODEL), D_MODEL ** -0.5),
        'expert_v': nrm(ks[21], (DEPTH, P_EXPERTS, D_MODEL), D_MODEL ** -0.5),
        'final_norm_w': 1.0 + nrm(ks[22], (D_MODEL,), 0.02),
    }


def reference(x, c, ctx, c_ctx, w_ada, b_ada, norm1_w, w_in, conv_w, a_log, dt_bias, gdn_norm_w,
              lb_logits, hg_norm_w, w_pa, w_pb, w_o, norm2_w, w_query, sub_keys, expert_u, expert_v,
              final_norm_w):
    lb_all = jnp.cumsum(jax.nn.softmax(lb_logits.astype(jnp.float32), axis=0), axis=0)
    xc = ctx
    for l in range(DEPTH):
        last = l == DEPTH - 1
        mod = jax.nn.silu(c) @ w_ada[l] + b_ada[l]
        mod_c = jax.nn.silu(c_ctx) @ w_ada[l] + b_ada[l]
        sh1, sc1, g1, sh2, sc2, g2 = jnp.split(mod[:, None, :], 6, axis=-1)
        sh1c, sc1c, g1c, sh2c, sc2c, g2c = jnp.split(mod_c, 6, axis=-1)
        h = _modulate(_rmsnorm(x, norm1_w[l]), sh1, sc1)
        hc = _modulate(_rmsnorm(xc, norm1_w[l]), sh1c, sc1c)
        mix, mix_c = _token_mixers(h, hc, not last, w_in[l], conv_w[l], a_log[l], dt_bias[l], gdn_norm_w[l],
                                   lb_all[l], hg_norm_w[l], w_pa[l], w_pb[l], w_o[l])
        x = x + g1 * mix
        x = x + g2 * _peer(_modulate(_rmsnorm(x, norm2_w[l]), sh2, sc2),
                           w_query[l], sub_keys[l], expert_u[l], expert_v[l])
        if not last:
            xc = xc + g1c * mix_c
            xc = xc + g2c * _peer(_modulate(_rmsnorm(xc, norm2_w[l]), sh2c, sc2c),
                                  w_query[l], sub_keys[l], expert_u[l], expert_v[l])
    return _rmsnorm(x, final_norm_w)
```

```python
import functools

import jax
import jax.numpy as jnp
from jax import lax
from jax.experimental import pallas as pl
from jax.experimental.pallas import tpu as pltpu

EPS = 1e-6
CHUNK = 64
SEG = 2 * CHUNK
GRID_W = 64
LANES = 128
HEADS = 8
HEAD_DIM = 128
P_HEADS = 8
P_NKEYS = 128
P_TOPK = 16
P_DQ = 128
PEER_TB = 8
PEER_K = P_HEADS * P_TOPK

HI = lax.Precision.HIGHEST
bf16 = jnp.bfloat16
f32 = jnp.float32


def _rmsnorm(x, w):
    return x * lax.rsqrt(jnp.mean(x * x, axis=-1, keepdims=True) + EPS) * w


def _dot(a, b):
    return jnp.dot(a.astype(bf16), b.astype(bf16), preferred_element_type=f32)


def _dot_nt(a, b):
    return lax.dot_general(a.astype(bf16), b.astype(bf16), (((1,), (1,)), ((), ())), preferred_element_type=f32)


def _dot_tn(a, b):
    return lax.dot_general(a.astype(bf16), b.astype(bf16), (((0,), (0,)), ((), ())), preferred_element_type=f32)


def _chunk_iota():
    r = lax.broadcasted_iota(jnp.int32, (CHUNK, CHUNK), 0)
    c = lax.broadcasted_iota(jnp.int32, (CHUNK, CHUNK), 1)
    return r, c


def _conv_kernel(x_ref, prev_ref, next_ref, w_ref, o_ref):
    i = pl.program_id(1)
    j = pl.program_id(2)
    x = x_ref[0]
    tm = x.shape[0]
    row = lax.broadcasted_iota(jnp.int32, x.shape, 0)
    prev_row = prev_ref[0, 7:8, :] * (i > 0).astype(f32)
    next_row = next_ref[0, 0:1, :] * (i < pl.num_programs(1) - 1).astype(f32)
    x_prev = jnp.where(row == 0, prev_row, pltpu.roll(x, 1, axis=0))
    x_next = jnp.where(row == tm - 1, next_row, pltpu.roll(x, tm - 1, axis=0))
    y = x_prev * w_ref[0:1, :] + x * w_ref[1:2, :] + x_next * w_ref[2:3, :]
    y = y * jax.nn.sigmoid(y)
    scale = jnp.where(j == 0, HEAD_DIM ** -0.5, 1.0).astype(f32)
    is_v = j == 2
    for h in range(HEADS):
        seg = y[:, h * HEAD_DIM:(h + 1) * HEAD_DIM]
        nrm = seg * lax.rsqrt(jnp.sum(seg * seg, axis=1, keepdims=True) + EPS) * scale
        o_ref[0, :, h * HEAD_DIM:(h + 1) * HEAD_DIM] = jnp.where(is_v, seg, nrm)


def _qkv_conv(p3, conv_w, tm):
    bsz, seq, _ = p3.shape
    width = conv_w.shape[1] // 3
    r8 = tm // 8
    return pl.pallas_call(
        _conv_kernel,
        grid=(bsz, seq // tm, 3),
        in_specs=[pl.BlockSpec((1, tm, width), lambda b, i, j: (b, i, j)),
                  pl.BlockSpec((1, 8, width), lambda b, i, j: (b, jnp.maximum(i * r8 - 1, 0), j)),
                  pl.BlockSpec((1, 8, width), lambda b, i, j: (b, jnp.minimum((i + 1) * r8, seq // 8 - 1), j)),
                  pl.BlockSpec((3, width), lambda b, i, j: (0, j))],
        out_specs=pl.BlockSpec((1, tm, width), lambda b, i, j: (b, i, j)),
        out_shape=jax.ShapeDtypeStruct((bsz, seq, 3 * width), f32),
        compiler_params=pltpu.CompilerParams(dimension_semantics=("parallel", "parallel", "parallel")),
        name="qkv_conv",
    )(p3, p3, p3, conv_w)


def _tri_inverse(m):
    r, c = _chunk_iota()
    eye = (r == c).astype(f32)
    md = jnp.where((r // 8) == (c // 8), m, 0.0)
    md2 = _dot(md, md)
    md4 = _dot(md2, md2)
    x = _dot(_dot(eye - md, eye + md2), eye + md4)
    b = 8
    while b < CHUNK:
        lvl = jnp.where(((r // (2 * b)) == (c // (2 * b))) & ((r // b) != (c // b)), m, 0.0)
        x = x - _dot(_dot(x, lvl), x)
        b *= 2
    return x


def _gdn_chunk(q, k, v, gcc, gcr, bcol, gl, state, reverse):
    r, c = _chunk_iota()
    incl = (r <= c) if reverse else (r >= c)
    strict = (r < c) if reverse else (r > c)
    decay = jnp.where(incl, jnp.exp(jnp.where(incl, gcc - gcr, 0.0)), 0.0)
    kb = k * bcol
    t = _tri_inverse(jnp.where(strict, _dot_nt(kb, k) * decay, 0.0))
    eg = jnp.exp(gcc)
    sol = _dot(t, jnp.concatenate([v * bcol, kb * eg], axis=1))
    u, w = sol[:, :HEAD_DIM], sol[:, HEAD_DIM:]
    qk = _dot_nt(q, k) * decay
    v_new = u - _dot(w, state)
    o = _dot(q * eg, state) + _dot(qk, v_new)
    state = jnp.exp(gl) * state + _dot_tn(k * jnp.exp(gl - gcc), v_new)
    return o, state


def _softplus(x):
    return jnp.maximum(x, 0.0) + jnp.log(1.0 + jnp.exp(-jnp.abs(x)))


def _gdn_kernel(alog_ref, dtb_ref, qf, kf, vf, abf, qb, kb, vb, abb, s0f, s0b,
                of_ref, ob_ref, sf_out, sb_out, sf, sb):
    h = pl.program_id(1)
    step = pl.program_id(2)

    @pl.when(step == 0)
    def _():
        sf[...] = s0f[0, 0]
        sb[...] = s0b[0, 0]

    jr = lax.broadcasted_iota(jnp.int32, (SEG, SEG), 0)
    ic = lax.broadcasted_iota(jnp.int32, (SEG, SEG), 1)
    same = (jr // CHUNK) == (ic // CHUNK)

    def prep(ab_ref, d):
        ab = ab_ref[0, 0]
        g = -jnp.exp(alog_ref[d, h]) * _softplus(ab[d:d + 1, :] + dtb_ref[d, h])
        beta = jax.nn.sigmoid(ab[2 + d:3 + d, :])
        cum = (same & ((jr >= ic) if d else (jr <= ic))).astype(f32)
        rows = jnp.concatenate([g, jnp.zeros((7, SEG), f32)], axis=0)
        gc = jnp.dot(rows, cum, precision=HI, preferred_element_type=f32)[0:1, :]
        cols = jnp.concatenate([gc, beta, jnp.zeros((SEG - 2, SEG), f32)], axis=0).T
        return gc, cols[:, 0:1], cols[:, 1:2]

    gcr_f, gcc_f, bc_f = prep(abf, 0)
    gcr_b, gcc_b, bc_b = prep(abb, 1)
    st_f = sf[...]
    st_b = sb[...]
    for half in range(SEG // CHUNK):
        lo = half * CHUNK
        sl = slice(lo, lo + CHUNK)
        o, st_f = _gdn_chunk(qf[0, sl, :], kf[0, sl, :], vf[0, sl, :], gcc_f[sl], gcr_f[:, sl], bc_f[sl],
                             gcr_f[:, lo + CHUNK - 1:lo + CHUNK], st_f, False)
        of_ref[0, sl, :] = o
        lo = SEG - CHUNK - half * CHUNK
        sl = slice(lo, lo + CHUNK)
        o, st_b = _gdn_chunk(qb[0, sl, :], kb[0, sl, :], vb[0, sl, :], gcc_b[sl], gcr_b[:, sl], bc_b[sl],
                             gcr_b[:, lo:lo + 1], st_b, True)
        ob_ref[0, sl, :] = o
    sf[...] = st_f
    sb[...] = st_b

    @pl.when(step == pl.num_programs(2) - 1)
    def _():
        sf_out[0, 0] = st_f
        sb_out[0, 0] = st_b


def _gdn_scan(qkv, abt, a_log, dt_bias, s0f, s0b):
    bsz, seq, _ = qkv.shape
    n_steps = seq // SEG
    pos = lambda s, rev: (n_steps - 1 - s) if rev else s
    in_spec = lambda cb, rev: pl.BlockSpec((1, SEG, HEAD_DIM), lambda b, h, s: (b, pos(s, rev), cb + h))
    ab_spec = lambda rev: pl.BlockSpec((1, 1, 4, SEG), lambda b, h, s: (b, h, 0, pos(s, rev)))
    out_spec = lambda rev: pl.BlockSpec((1, SEG, HEAD_DIM), lambda b, h, s: (b, pos(s, rev), h))
    smem = pl.BlockSpec(memory_space=pltpu.SMEM)
    st_spec = pl.BlockSpec((1, 1, HEAD_DIM, HEAD_DIM), lambda b, h, s: (b, h, 0, 0))
    o_shape = jax.ShapeDtypeStruct((bsz, seq, HEADS * HEAD_DIM), f32)
    st_shape = jax.ShapeDtypeStruct((bsz, HEADS, HEAD_DIM, HEAD_DIM), f32)
    return pl.pallas_call(
        _gdn_kernel,
        grid=(bsz, HEADS, n_steps),
        in_specs=[smem, smem,
                  in_spec(0, False), in_spec(HEADS, False), in_spec(2 * HEADS, False), ab_spec(False),
                  in_spec(0, True), in_spec(HEADS, True), in_spec(2 * HEADS, True), ab_spec(True),
                  st_spec, st_spec],
        out_specs=[out_spec(False), out_spec(True), st_spec, st_spec],
        out_shape=[o_shape, o_shape, st_shape, st_shape],
        scratch_shapes=[pltpu.VMEM((HEAD_DIM, HEAD_DIM), f32), pltpu.VMEM((HEAD_DIM, HEAD_DIM), f32)],
        compiler_params=pltpu.CompilerParams(dimension_semantics=("parallel", "parallel", "arbitrary")),
        name="gdn_scan",
    )(a_log, dt_bias, qkv, qkv, qkv, abt, qkv, qkv, qkv, abt, s0f, s0b)


def _gla_chunk(q_raw, f_raw, v, lb, st, reverse):
    r, c = _chunk_iota()
    mask = (r <= c) if reverse else (r >= c)
    q = q_raw * jax.nn.sigmoid(q_raw) * (HEAD_DIM ** -0.5)
    f = lb + (1.0 - lb) * jax.nn.sigmoid(f_raw)
    logf = jnp.log(f)
    k = 1.0 - f
    bc = jnp.dot(mask.astype(f32), logf, precision=HI, preferred_element_type=f32)
    ref_i = CHUNK // 2 - 1 if reverse else CHUNK // 2
    last_i = 0 if reverse else CHUNK - 1
    ref = bc[ref_i:ref_i + 1, :]
    last = bc[last_i:last_i + 1, :]
    sc = jnp.where(mask, _dot_nt(q * jnp.exp(bc - ref), k * jnp.exp(ref - bc)), 0.0)
    o = _dot(sc, v) + _dot_nt(q * jnp.exp(bc), st)
    st = st * jnp.exp(last) + _dot_tn(v, k * jnp.exp(last - bc))
    return o, st


def _gla_kernel(qf, ff, vf, qb, fb, vb, lb_ref, s0f, s0b, of_ref, ob_ref, sf_out, sb_out, sf, sb):
    step = pl.program_id(2)

    @pl.when(step == 0)
    def _():
        sf[...] = s0f[0, 0]
        sb[...] = s0b[0, 0]

    lb = lb_ref[...]
    st_f = sf[...]
    st_b = sb[...]
    for half in range(SEG // CHUNK):
        sl = slice(half * CHUNK, (half + 1) * CHUNK)
        o, st_f = _gla_chunk(qf[0, sl, :], ff[0, sl, :], vf[0, sl, :], lb, st_f, False)
        of_ref[0, sl, :] = o
        sl = slice(SEG - (half + 1) * CHUNK, SEG - half * CHUNK)
        o, st_b = _gla_chunk(qb[0, sl, :], fb[0, sl, :], vb[0, sl, :], lb, st_b, True)
        ob_ref[0, sl, :] = o
    sf[...] = st_f
    sb[...] = st_b

    @pl.when(step == pl.num_programs(2) - 1)
    def _():
        sf_out[0, 0] = st_f
        sb_out[0, 0] = st_b


def _gla_scan(p3, col_blocks, lb, s0f, s0b, n_steps, column_major):
    bsz = p3.shape[0]
    cq, cff, cfb, cv = col_blocks
    pos = lambda s, rev: (n_steps - 1 - s) if rev else s
    if column_major:
        lane_blocks = p3.shape[2] // (n_steps * LANES)
        in_spec = lambda cb, rev: pl.BlockSpec((1, SEG, LANES), lambda b, h, s: (b, 0, pos(s, rev) * lane_blocks + cb + h))
        out_spec = lambda rev: pl.BlockSpec((1, SEG, LANES), lambda b, h, s: (b, 0, pos(s, rev) * HEADS + h))
        o_shape = jax.ShapeDtypeStruct((bsz, SEG, n_steps * HEADS * HEAD_DIM), f32)
    else:
        in_spec = lambda cb, rev: pl.BlockSpec((1, SEG, LANES), lambda b, h, s: (b, pos(s, rev), cb + h))
        out_spec = lambda rev: pl.BlockSpec((1, SEG, LANES), lambda b, h, s: (b, pos(s, rev), h))
        o_shape = jax.ShapeDtypeStruct((bsz, n_steps * SEG, HEADS * HEAD_DIM), f32)
    st_spec = pl.BlockSpec((1, 1, HEAD_DIM, HEAD_DIM), lambda b, h, s: (b, h, 0, 0))
    st_shape = jax.ShapeDtypeStruct((bsz, HEADS, HEAD_DIM, HEAD_DIM), f32)
    return pl.pallas_call(
        _gla_kernel,
        grid=(bsz, HEADS, n_steps),
        in_specs=[in_spec(cq, False), in_spec(cff, False), in_spec(cv, False),
                  in_spec(cq, True), in_spec(cfb, True), in_spec(cv, True),
                  pl.BlockSpec((1, HEAD_DIM), lambda b, h, s: (0, h)), st_spec, st_spec],
        out_specs=[out_spec(False), out_spec(True), st_spec, st_spec],
        out_shape=[o_shape, o_shape, st_shape, st_shape],
        scratch_shapes=[pltpu.VMEM((HEAD_DIM, HEAD_DIM), f32), pltpu.VMEM((HEAD_DIM, HEAD_DIM), f32)],
        compiler_params=pltpu.CompilerParams(dimension_semantics=("parallel", "parallel", "arbitrary")),
        name="hgrn2_scan",
    )(p3, p3, p3, p3, p3, p3, lb.reshape(1, -1), s0f, s0b)


_W = HEADS * HEAD_DIM
C_QKV, C_GA, C_QB, C_FF, C_FB, C_IB, C_GB, C_MG, C_AB = (0, 3 * _W, 4 * _W, 5 * _W, 6 * _W, 7 * _W, 8 * _W, 9 * _W, 11 * _W)
N_PAD = 11 * _W + LANES


def _reorder_w_in(w_in):
    ab0 = 4 * _W
    ab1 = ab0 + 4 * HEADS
    pad = jnp.zeros((w_in.shape[0], LANES - 4 * HEADS), w_in.dtype)
    return jnp.concatenate([w_in[:, :ab0], w_in[:, ab1:], w_in[:, ab0:ab1], pad], axis=1)


def _mixer_scans(p, p_ctx, conv_w, a_log, dt_bias, lb):
    bsz, seq, _ = p.shape
    ctx_len = p_ctx.shape[1]
    rows = seq // GRID_W
    zeros = jnp.zeros((bsz, HEADS, HEAD_DIM, HEAD_DIM), f32)
    abt = lambda t: t[:, :, C_AB:C_AB + 4 * HEADS].reshape(bsz, -1, 4, HEADS).transpose(0, 3, 2, 1)
    _, _, sa_f, sa_b = _gdn_scan(_qkv_conv(p_ctx, conv_w, ctx_len), abt(p_ctx), a_log, dt_bias, zeros, zeros)
    oa_f, oa_b, _, _ = _gdn_scan(_qkv_conv(p, conv_w, 512), abt(p), a_log, dt_bias, sa_f, sa_b)
    cols = tuple(c // LANES for c in (C_QB, C_FF, C_FB, C_IB))
    _, _, sb_f, sb_b = _gla_scan(p_ctx, cols, lb, zeros, zeros, ctx_len // SEG, False)
    ob_f, ob_b, _, _ = _gla_scan(p.reshape(bsz, rows, GRID_W * N_PAD), cols, lb, sb_f, sb_b, GRID_W, True)
    flat = lambda t: t.reshape(bsz * seq, _W)
    return flat(oa_f), flat(oa_b), flat(ob_f), flat(ob_b)


def _token_mixers(h, hc, w_in, conv_w, a_log, dt_bias, gdn_norm_w, lb, hg_norm_w, w_pa, w_pb, w_o):
    bsz, seq, d = h.shape
    assert seq // GRID_W == SEG, "the column-major scan reads one grid column (SEG rows) per step"
    w_r = _reorder_w_in(w_in)
    p = h @ w_r
    p_ctx = hc @ w_r
    oa_f, oa_b, ob_f, ob_b = _mixer_scans(p, p_ctx, conv_w, a_log, dt_bias, lb)
    p2 = p.reshape(bsz * seq, N_PAD)

    def head_out(o, norm_w, gate):
        o = _rmsnorm(o.reshape(-1, HEADS, HEAD_DIM), norm_w).reshape(-1, _W)
        return o * jax.nn.silu(gate)

    o_a = head_out(oa_f + oa_b, gdn_norm_w, p2[:, C_GA:C_GA + _W])
    o_b = head_out(ob_f + ob_b, hg_norm_w, p2[:, C_GB:C_GB + _W])
    g_a = jax.nn.sigmoid(p2[:, C_MG:C_MG + d])
    g_b = jax.nn.sigmoid(p2[:, C_MG + d:C_MG + 2 * d])
    return ((g_a * (o_a @ w_pa) + g_b * (o_b @ w_pb)) @ w_o).reshape(bsz, seq, d)


def _peer_expert_kernel(idx_cur, idx_nxt, t_ref, g_ref, uv_hbm, o_ref, buf, sems, s_scr, a_scr):
    i = pl.program_id(0)
    n = pl.num_programs(0)
    d = t_ref.shape[1]
    rows = PEER_TB * PEER_K

    def row_copy(idx_ref, slot, j, k):
        e = idx_ref[j, k]
        return pltpu.make_async_copy(uv_hbm.at[pl.ds(e, 1)], buf.at[slot, pl.ds(j * PEER_K + k, 1)],
                                     sems.at[slot])

    def issue(idx_ref, slot):
        def tok(j, carry):
            for k in range(PEER_K):
                row_copy(idx_ref, slot, j, k).start()
            return carry
        lax.fori_loop(0, PEER_TB, tok, 0)

    @pl.when(i == 0)
    def _():
        issue(idx_cur, 0)

    @pl.when(i + 1 < n)
    def _():
        issue(idx_nxt, (i + 1) % 2)

    slot = i % 2
    pltpu.make_async_copy(uv_hbm.at[pl.ds(0, rows)], buf.at[slot], sems.at[slot]).wait()

    n_tiles = d // LANES
    for j in range(PEER_TB):
        part = None
        for c in range(n_tiles):
            u_c = buf[slot, pl.ds(j * PEER_K, PEER_K), pl.ds(c * LANES, LANES)]
            p = u_c * t_ref[pl.ds(j, 1), pl.ds(c * LANES, LANES)]
            part = p if part is None else part + p
        s_scr[:, pl.ds(j, 1)] = jnp.sum(part, axis=1, keepdims=True)
    s = s_scr[...]
    a_scr[...] = 0.5 * s * (1.0 + lax.erf(s * (2.0 ** -0.5))) * g_ref[0]
    for j in range(PEER_TB):
        a_col = a_scr[:, pl.ds(j, 1)]
        for c in range(n_tiles):
            v_c = buf[slot, pl.ds(j * PEER_K, PEER_K), pl.ds(d + c * LANES, LANES)]
            o_ref[pl.ds(j, 1), pl.ds(c * LANES, LANES)] = jnp.sum(a_col * v_c, axis=0, keepdims=True)


def _peer_experts(t, experts, gates, expert_u, expert_v):
    n_tok, d = t.shape
    nb = n_tok // PEER_TB
    uv = jnp.concatenate([expert_u, expert_v], axis=1)
    gates_t = gates.reshape(nb, PEER_TB, PEER_K).transpose(0, 2, 1)
    smem_blk = lambda im: pl.BlockSpec((PEER_TB, PEER_K), im, memory_space=pltpu.SMEM)
    return pl.pallas_call(
        _peer_expert_kernel,
        grid=(nb,),
        in_specs=[smem_blk(lambda i: (i, 0)),
                  smem_blk(lambda i: (jnp.minimum(i + 1, nb - 1), 0)),
                  pl.BlockSpec((PEER_TB, d), lambda i: (i, 0)),
                  pl.BlockSpec((1, PEER_K, PEER_TB), lambda i: (i, 0, 0)),
                  pl.BlockSpec(memory_space=pl.ANY)],
        out_specs=pl.BlockSpec((PEER_TB, d), lambda i: (i, 0)),
        out_shape=jax.ShapeDtypeStruct((n_tok, d), f32),
        scratch_shapes=[pltpu.VMEM((2, PEER_TB * PEER_K, 2 * d), f32),
                        pltpu.SemaphoreType.DMA((2,)),
                        pltpu.VMEM((PEER_K, PEER_TB), f32),
                        pltpu.VMEM((PEER_K, PEER_TB), f32)],
        compiler_params=pltpu.CompilerParams(dimension_semantics=("arbitrary",),
                                             vmem_limit_bytes=40 * 1024 * 1024),
        name="peer_experts",
    )(experts, experts, t, gates_t, uv)


def _peer(h, w_query, sub_keys, expert_u, expert_v):
    b, s, d = h.shape
    n_tok = b * s
    t = h.reshape(n_tok, d)
    q = (t @ w_query).reshape(n_tok, P_HEADS, 2, P_DQ)
    sc = jnp.einsum('thcd,hckd->thck', q, sub_keys)
    s1, i1 = lax.top_k(sc[:, :, 0], P_TOPK)
    s2, i2 = lax.top_k(sc[:, :, 1], P_TOPK)
    cand = (s1[..., :, None] + s2[..., None, :]).reshape(n_tok, P_HEADS, P_TOPK * P_TOPK)
    top_s, top_i = lax.top_k(cand, P_TOPK)
    e1 = jnp.take_along_axis(i1, top_i // P_TOPK, axis=-1)
    e2 = jnp.take_along_axis(i2, top_i % P_TOPK, axis=-1)
    experts = (e1 * P_NKEYS + e2).reshape(n_tok, PEER_K)
    gates = jax.nn.softmax(top_s.astype(f32), axis=-1).reshape(n_tok, PEER_K)
    y = _peer_experts(t, experts, gates, expert_u, expert_v)
    return y.reshape(b, s, d)


def _final_norm_kernel(x_ref, w_ref, o_ref):
    x = x_ref[...]
    ms = jnp.mean(x * x, axis=-1, keepdims=True)
    o_ref[...] = x * lax.rsqrt(ms + EPS) * w_ref[...]


def _final_norm(x2d, w):
    n, d = x2d.shape
    tm = 512
    return pl.pallas_call(
        _final_norm_kernel,
        grid=(n // tm,),
        in_specs=[pl.BlockSpec((tm, d), lambda i: (i, 0)), pl.BlockSpec((1, d), lambda i: (0, 0))],
        out_specs=pl.BlockSpec((tm, d), lambda i: (i, 0)),
        out_shape=jax.ShapeDtypeStruct((n, d), f32),
        name="final_norm",
    )(x2d, w.reshape(1, d))


def kernel(x, c, ctx, c_ctx, w_ada, b_ada, norm1_w, w_in, conv_w, a_log, dt_bias, gdn_norm_w, lb_logits,
           hg_norm_w, w_pa, w_pb, w_o, norm2_w, w_query, sub_keys, expert_u, expert_v, final_norm_w):
    assert w_ada.shape[0] == 1, "single-layer block"
    lb_all = jnp.cumsum(jax.nn.softmax(lb_logits.astype(f32), axis=0), axis=0)
    l = 0
    mod = jax.nn.silu(c) @ w_ada[l] + b_ada[l]
    mod_c = jax.nn.silu(c_ctx) @ w_ada[l] + b_ada[l]
    sh1, sc1, g1, sh2, sc2, g2 = jnp.split(mod[:, None, :], 6, axis=-1)
    sh1c, sc1c, _, _, _, _ = jnp.split(mod_c, 6, axis=-1)
    h = _rmsnorm(x, norm1_w[l]) * (1 + sc1) + sh1
    hc = _rmsnorm(ctx, norm1_w[l]) * (1 + sc1c) + sh1c
    mix = _token_mixers(h, hc, w_in[l], conv_w[l], a_log[l], dt_bias[l], gdn_norm_w[l],
                        lb_all[l], hg_norm_w[l], w_pa[l], w_pb[l], w_o[l])
    x = x + g1 * mix
    x = x + g2 * _peer(_rmsnorm(x, norm2_w[l]) * (1 + sc2) + sh2,
                       w_query[l], sub_keys[l], expert_u[l], expert_v[l])
    b, s, d = x.shape
    return _final_norm(x.reshape(b * s, d), final_norm_w).reshape(b, s, d)
```

```python
import functools

import jax
import jax.numpy as jnp
from jax import lax
from jax.experimental import pallas as pl
from jax.experimental.pallas import tpu as pltpu

EPS = 1e-6
CHUNK = 64
SEG = 2 * CHUNK
GDN_NC = 8
GRID_W = 64
LANES = 128
HEADS = 8
HEAD_DIM = 128
P_HEADS = 8
P_NKEYS = 128
P_TOPK = 16
P_DQ = 128
PEER_TB = 8
PEER_K = P_HEADS * P_TOPK

HI = lax.Precision.HIGHEST
bf16 = jnp.bfloat16
f32 = jnp.float32


def _rmsnorm(x, w):
    return x * lax.rsqrt(jnp.mean(x * x, axis=-1, keepdims=True) + EPS) * w


def _dot(a, b):
    return jnp.dot(a.astype(bf16), b.astype(bf16), preferred_element_type=f32)


def _dot_nt(a, b):
    return lax.dot_general(a.astype(bf16), b.astype(bf16), (((1,), (1,)), ((), ())), preferred_element_type=f32)


def _dot_tn(a, b):
    return lax.dot_general(a.astype(bf16), b.astype(bf16), (((0,), (0,)), ((), ())), preferred_element_type=f32)


def _chunk_iota():
    r = lax.broadcasted_iota(jnp.int32, (CHUNK, CHUNK), 0)
    c = lax.broadcasted_iota(jnp.int32, (CHUNK, CHUNK), 1)
    return r, c


def _conv_kernel(x_ref, prev_ref, next_ref, w_ref, o_ref):
    i = pl.program_id(1)
    j = pl.program_id(2)
    x = x_ref[0]
    tm = x.shape[0]
    row = lax.broadcasted_iota(jnp.int32, x.shape, 0)
    prev_row = prev_ref[0, 7:8, :] * (i > 0).astype(f32)
    next_row = next_ref[0, 0:1, :] * (i < pl.num_programs(1) - 1).astype(f32)
    x_prev = jnp.where(row == 0, prev_row, pltpu.roll(x, 1, axis=0))
    x_next = jnp.where(row == tm - 1, next_row, pltpu.roll(x, tm - 1, axis=0))
    y = x_prev * w_ref[0:1, :] + x * w_ref[1:2, :] + x_next * w_ref[2:3, :]
    y = y * jax.nn.sigmoid(y)
    scale = jnp.where(j == 0, HEAD_DIM ** -0.5, 1.0).astype(f32)
    is_v = j == 2
    for h in range(HEADS):
        seg = y[:, h * HEAD_DIM:(h + 1) * HEAD_DIM]
        nrm = seg * lax.rsqrt(jnp.sum(seg * seg, axis=1, keepdims=True) + EPS) * scale
        o_ref[0, :, h * HEAD_DIM:(h + 1) * HEAD_DIM] = jnp.where(is_v, seg, nrm)


def _qkv_conv(p3, conv_w, tm):
    bsz, seq, _ = p3.shape
    width = conv_w.shape[1] // 3
    r8 = tm // 8
    return pl.pallas_call(
        _conv_kernel,
        grid=(bsz, seq // tm, 3),
        in_specs=[pl.BlockSpec((1, tm, width), lambda b, i, j: (b, i, j)),
                  pl.BlockSpec((1, 8, width), lambda b, i, j: (b, jnp.maximum(i * r8 - 1, 0), j)),
                  pl.BlockSpec((1, 8, width), lambda b, i, j: (b, jnp.minimum((i + 1) * r8, seq // 8 - 1), j)),
                  pl.BlockSpec((3, width), lambda b, i, j: (0, j))],
        out_specs=pl.BlockSpec((1, tm, width), lambda b, i, j: (b, i, j)),
        out_shape=jax.ShapeDtypeStruct((bsz, seq, 3 * width), f32),
        compiler_params=pltpu.CompilerParams(dimension_semantics=("parallel", "parallel", "parallel")),
        name="qkv_conv",
    )(p3, p3, p3, conv_w)


def _tri_inverse_many(ms):
    r, c = _chunk_iota()
    eye = (r == c).astype(f32)
    blk = (r // 8) == (c // 8)
    md = [jnp.where(blk, m, 0.0) for m in ms]
    md2 = [_dot(a, a) for a in md]
    md4 = [_dot(a, a) for a in md2]
    x = [_dot(eye - a, eye + b) for a, b in zip(md, md2)]
    x = [_dot(a, eye + b) for a, b in zip(x, md4)]
    b = 8
    while b < CHUNK:
        sel = ((r // (2 * b)) == (c // (2 * b))) & ((r // b) != (c // b))
        xl = [_dot(a, jnp.where(sel, m, 0.0)) for a, m in zip(x, ms)]
        x = [a - _dot(t, a) for a, t in zip(x, xl)]
        b *= 2
    return x


def _softplus(x):
    return jnp.maximum(x, 0.0) + jnp.log(1.0 + jnp.exp(-jnp.abs(x)))


def _gdn_kernel(alog_ref, dtb_ref, qf, kf, vf, abf, qb, kb, vb, abb, s0f, s0b,
                of_ref, ob_ref, sf_out, sb_out, sf, sb, *, nc):
    seg = nc * CHUNK
    h = pl.program_id(1)
    step = pl.program_id(2)

    @pl.when(step == 0)
    def _():
        sf[...] = s0f[0, 0]
        sb[...] = s0b[0, 0]

    jr = lax.broadcasted_iota(jnp.int32, (seg, seg), 0)
    ic = lax.broadcasted_iota(jnp.int32, (seg, seg), 1)
    same = (jr // CHUNK) == (ic // CHUNK)
    r, c = _chunk_iota()

    def prep(ab_ref, d):
        ab = ab_ref[0, 0]
        g = -jnp.exp(alog_ref[d, h]) * _softplus(ab[d:d + 1, :] + dtb_ref[d, h])
        beta = jax.nn.sigmoid(ab[2 + d:3 + d, :])
        cum = (same & ((jr >= ic) if d else (jr <= ic))).astype(f32)
        rows = jnp.concatenate([g, jnp.zeros((7, seg), f32)], axis=0)
        gc = jnp.dot(rows, cum, precision=HI, preferred_element_type=f32)[0:1, :]
        cols = []
        for p in range(seg // LANES):
            sl = slice(p * LANES, (p + 1) * LANES)
            cols.append(jnp.concatenate([gc[:, sl], beta[:, sl], jnp.zeros((LANES - 2, LANES), f32)], axis=0).T)
        cols = jnp.concatenate(cols, axis=0)
        return gc, cols[:, 0:1], cols[:, 1:2]

    inst = []
    for d, (q_ref, k_ref, v_ref, ab_ref) in enumerate(((qf, kf, vf, abf), (qb, kb, vb, abb))):
        gcr, gcc, bcol = prep(ab_ref, d)
        for ci in range(nc):
            sl = slice(ci * CHUNK, (ci + 1) * CHUNK)
            gl = gcr[:, ci * CHUNK:ci * CHUNK + 1] if d else gcr[:, (ci + 1) * CHUNK - 1:(ci + 1) * CHUNK]
            inst.append(dict(d=d, sl=sl, q=q_ref[0, sl, :], k=k_ref[0, sl, :], v=v_ref[0, sl, :],
                             gcc=gcc[sl], gcr=gcr[:, sl], b=bcol[sl], gl=gl))
    for t in inst:
        incl = (r <= c) if t['d'] else (r >= c)
        t['strict'] = (r < c) if t['d'] else (r > c)
        t['decay'] = jnp.where(incl, jnp.exp(jnp.where(incl, t['gcc'] - t['gcr'], 0.0)), 0.0)
        t['kb'] = t['k'] * t['b']
        t['eg'] = jnp.exp(t['gcc'])
    kk = [_dot_nt(t['kb'], t['k']) for t in inst]
    qk = [_dot_nt(t['q'], t['k']) for t in inst]
    tinv = _tri_inverse_many([jnp.where(t['strict'], a * t['decay'], 0.0) for t, a in zip(inst, kk)])
    sol = [_dot(x, jnp.concatenate([t['v'] * t['b'], t['kb'] * t['eg']], axis=1)) for t, x in zip(inst, tinv)]
    for t, s_, a in zip(inst, sol, qk):
        t['u'], t['w'] = s_[:, :HEAD_DIM], s_[:, HEAD_DIM:]
        t['qk'] = a * t['decay']
        t['qd'] = t['q'] * t['eg']
        t['kd'] = t['k'] * jnp.exp(t['gl'] - t['gcc'])
        t['tot'] = jnp.exp(t['gl'])

    st = [sf[...], sb[...]]
    fwd = [t for t in inst if t['d'] == 0]
    bwd = [t for t in inst if t['d'] == 1][::-1]
    for tf, tb in zip(fwd, bwd):
        for t, o_ref in ((tf, of_ref), (tb, ob_ref)):
            s_ = st[t['d']]
            v_new = t['u'] - _dot(t['w'], s_)
            o_ref[0, t['sl'], :] = _dot(t['qd'], s_) + _dot(t['qk'], v_new)
            st[t['d']] = t['tot'] * s_ + _dot_tn(t['kd'], v_new)
    sf[...] = st[0]
    sb[...] = st[1]

    @pl.when(step == pl.num_programs(2) - 1)
    def _():
        sf_out[0, 0] = st[0]
        sb_out[0, 0] = st[1]


def _gdn_scan(qkv, abt, a_log, dt_bias, s0f, s0b, nc):
    bsz, seq, _ = qkv.shape
    seg = nc * CHUNK
    n_steps = seq // seg
    pos = lambda s, rev: (n_steps - 1 - s) if rev else s
    in_spec = lambda cb, rev: pl.BlockSpec((1, seg, HEAD_DIM), lambda b, h, s: (b, pos(s, rev), cb + h))
    ab_spec = lambda rev: pl.BlockSpec((1, 1, 4, seg), lambda b, h, s: (b, h, 0, pos(s, rev)))
    out_spec = lambda rev: pl.BlockSpec((1, seg, HEAD_DIM), lambda b, h, s: (b, pos(s, rev), h))
    smem = pl.BlockSpec(memory_space=pltpu.SMEM)
    st_spec = pl.BlockSpec((1, 1, HEAD_DIM, HEAD_DIM), lambda b, h, s: (b, h, 0, 0))
    o_shape = jax.ShapeDtypeStruct((bsz, seq, HEADS * HEAD_DIM), f32)
    st_shape = jax.ShapeDtypeStruct((bsz, HEADS, HEAD_DIM, HEAD_DIM), f32)
    return pl.pallas_call(
        functools.partial(_gdn_kernel, nc=nc),
        grid=(bsz, HEADS, n_steps),
        in_specs=[smem, smem,
                  in_spec(0, False), in_spec(HEADS, False), in_spec(2 * HEADS, False), ab_spec(False),
                  in_spec(0, True), in_spec(HEADS, True), in_spec(2 * HEADS, True), ab_spec(True),
                  st_spec, st_spec],
        out_specs=[out_spec(False), out_spec(True), st_spec, st_spec],
        out_shape=[o_shape, o_shape, st_shape, st_shape],
        scratch_shapes=[pltpu.VMEM((HEAD_DIM, HEAD_DIM), f32), pltpu.VMEM((HEAD_DIM, HEAD_DIM), f32)],
        compiler_params=pltpu.CompilerParams(dimension_semantics=("parallel", "parallel", "arbitrary")),
        name="gdn_scan",
    )(a_log, dt_bias, qkv, qkv, qkv, abt, qkv, qkv, qkv, abt, s0f, s0b)


def _gla_chunk(q_raw, f_raw, v, lb, st, reverse):
    r, c = _chunk_iota()
    mask = (r <= c) if reverse else (r >= c)
    q = q_raw * jax.nn.sigmoid(q_raw) * (HEAD_DIM ** -0.5)
    f = lb + (1.0 - lb) * jax.nn.sigmoid(f_raw)
    logf = jnp.log(f)
    k = 1.0 - f
    bc = jnp.dot(mask.astype(f32), logf, precision=HI, preferred_element_type=f32)
    ref_i = CHUNK // 2 - 1 if reverse else CHUNK // 2
    last_i = 0 if reverse else CHUNK - 1
    ref = bc[ref_i:ref_i + 1, :]
    last = bc[last_i:last_i + 1, :]
    sc = jnp.where(mask, _dot_nt(q * jnp.exp(bc - ref), k * jnp.exp(ref - bc)), 0.0)
    o = _dot(sc, v) + _dot_nt(q * jnp.exp(bc), st)
    st = st * jnp.exp(last) + _dot_tn(v, k * jnp.exp(last - bc))
    return o, st


def _gla_kernel(qf, ff, vf, qb, fb, vb, lb_ref, s0f, s0b, of_ref, ob_ref, sf_out, sb_out, sf, sb):
    step = pl.program_id(2)

    @pl.when(step == 0)
    def _():
        sf[...] = s0f[0, 0]
        sb[...] = s0b[0, 0]

    lb = lb_ref[...]
    st_f = sf[...]
    st_b = sb[...]
    for half in range(SEG // CHUNK):
        sl = slice(half * CHUNK, (half + 1) * CHUNK)
        o, st_f = _gla_chunk(qf[0, sl, :], ff[0, sl, :], vf[0, sl, :], lb, st_f, False)
        of_ref[0, sl, :] = o
        sl = slice(SEG - (half + 1) * CHUNK, SEG - half * CHUNK)
        o, st_b = _gla_chunk(qb[0, sl, :], fb[0, sl, :], vb[0, sl, :], lb, st_b, True)
        ob_ref[0, sl, :] = o
    sf[...] = st_f
    sb[...] = st_b

    @pl.when(step == pl.num_programs(2) - 1)
    def _():
        sf_out[0, 0] = st_f
        sb_out[0, 0] = st_b


def _gla_scan(p3, col_blocks, lb, s0f, s0b, n_steps, column_major):
    bsz = p3.shape[0]
    cq, cff, cfb, cv = col_blocks
    pos = lambda s, rev: (n_steps - 1 - s) if rev else s
    if column_major:
        lane_blocks = p3.shape[2] // (n_steps * LANES)
        in_spec = lambda cb, rev: pl.BlockSpec((1, SEG, LANES), lambda b, h, s: (b, 0, pos(s, rev) * lane_blocks + cb + h))
        out_spec = lambda rev: pl.BlockSpec((1, SEG, LANES), lambda b, h, s: (b, 0, pos(s, rev) * HEADS + h))
        o_shape = jax.ShapeDtypeStruct((bsz, SEG, n_steps * HEADS * HEAD_DIM), f32)
    else:
        in_spec = lambda cb, rev: pl.BlockSpec((1, SEG, LANES), lambda b, h, s: (b, pos(s, rev), cb + h))
        out_spec = lambda rev: pl.BlockSpec((1, SEG, LANES), lambda b, h, s: (b, pos(s, rev), h))
        o_shape = jax.ShapeDtypeStruct((bsz, n_steps * SEG, HEADS * HEAD_DIM), f32)
    st_spec = pl.BlockSpec((1, 1, HEAD_DIM, HEAD_DIM), lambda b, h, s: (b, h, 0, 0))
    st_shape = jax.ShapeDtypeStruct((bsz, HEADS, HEAD_DIM, HEAD_DIM), f32)
    return pl.pallas_call(
        _gla_kernel,
        grid=(bsz, HEADS, n_steps),
        in_specs=[in_spec(cq, False), in_spec(cff, False), in_spec(cv, False),
                  in_spec(cq, True), in_spec(cfb, True), in_spec(cv, True),
                  pl.BlockSpec((1, HEAD_DIM), lambda b, h, s: (0, h)), st_spec, st_spec],
        out_specs=[out_spec(False), out_spec(True), st_spec, st_spec],
        out_shape=[o_shape, o_shape, st_shape, st_shape],
        scratch_shapes=[pltpu.VMEM((HEAD_DIM, HEAD_DIM), f32), pltpu.VMEM((HEAD_DIM, HEAD_DIM), f32)],
        compiler_params=pltpu.CompilerParams(dimension_semantics=("parallel", "parallel", "arbitrary")),
        name="hgrn2_scan",
    )(p3, p3, p3, p3, p3, p3, lb.reshape(1, -1), s0f, s0b)


_W = HEADS * HEAD_DIM
C_QKV, C_GA, C_QB, C_FF, C_FB, C_IB, C_GB, C_MG, C_AB = (0, 3 * _W, 4 * _W, 5 * _W, 6 * _W, 7 * _W, 8 * _W, 9 * _W, 11 * _W)
N_PAD = 11 * _W + LANES


def _reorder_w_in(w_in):
    ab0 = 4 * _W
    ab1 = ab0 + 4 * HEADS
    pad = jnp.zeros((w_in.shape[0], LANES - 4 * HEADS), w_in.dtype)
    return jnp.concatenate([w_in[:, :ab0], w_in[:, ab1:], w_in[:, ab0:ab1], pad], axis=1)


def _mixer_scans(p, p_ctx, conv_w, a_log, dt_bias, lb):
    bsz, seq, _ = p.shape
    ctx_len = p_ctx.shape[1]
    rows = seq // GRID_W
    zeros = jnp.zeros((bsz, HEADS, HEAD_DIM, HEAD_DIM), f32)
    abt = lambda t: t[:, :, C_AB:C_AB + 4 * HEADS].reshape(bsz, -1, 4, HEADS).transpose(0, 3, 2, 1)
    _, _, sa_f, sa_b = _gdn_scan(_qkv_conv(p_ctx, conv_w, ctx_len), abt(p_ctx), a_log, dt_bias, zeros, zeros,
                                 min(GDN_NC, ctx_len // CHUNK))
    oa_f, oa_b, _, _ = _gdn_scan(_qkv_conv(p, conv_w, 512), abt(p), a_log, dt_bias, sa_f, sa_b, GDN_NC)
    cols = tuple(c // LANES for c in (C_QB, C_FF, C_FB, C_IB))
    _, _, sb_f, sb_b = _gla_scan(p_ctx, cols, lb, zeros, zeros, ctx_len // SEG, False)
    ob_f, ob_b, _, _ = _gla_scan(p.reshape(bsz, rows, GRID_W * N_PAD), cols, lb, sb_f, sb_b, GRID_W, True)
    flat = lambda t: t.reshape(bsz * seq, _W)
    return flat(oa_f), flat(oa_b), flat(ob_f), flat(ob_b)


def _token_mixers(h, hc, w_in, conv_w, a_log, dt_bias, gdn_norm_w, lb, hg_norm_w, w_pa, w_pb, w_o):
    bsz, seq, d = h.shape
    assert seq // GRID_W == SEG, "the column-major scan reads one grid column (SEG rows) per step"
    w_r = _reorder_w_in(w_in)
    p = h @ w_r
    p_ctx = hc @ w_r
    oa_f, oa_b, ob_f, ob_b = _mixer_scans(p, p_ctx, conv_w, a_log, dt_bias, lb)
    p2 = p.reshape(bsz * seq, N_PAD)

    def head_out(o, norm_w, gate):
        o = _rmsnorm(o.reshape(-1, HEADS, HEAD_DIM), norm_w).reshape(-1, _W)
        return o * jax.nn.silu(gate)

    o_a = head_out(oa_f + oa_b, gdn_norm_w, p2[:, C_GA:C_GA + _W])
    o_b = head_out(ob_f + ob_b, hg_norm_w, p2[:, C_GB:C_GB + _W])
    g_a = jax.nn.sigmoid(p2[:, C_MG:C_MG + d])
    g_b = jax.nn.sigmoid(p2[:, C_MG + d:C_MG + 2 * d])
    return ((g_a * (o_a @ w_pa) + g_b * (o_b @ w_pb)) @ w_o).reshape(bsz, seq, d)


def _peer_expert_kernel(idx_cur, idx_nxt, t_ref, g_ref, uv_hbm, o_ref, buf, sems, s_scr, a_scr):
    i = pl.program_id(0)
    n = pl.num_programs(0)
    d = t_ref.shape[1]
    rows = PEER_TB * PEER_K

    def row_copy(idx_ref, slot, j, k):
        e = idx_ref[j, k]
        return pltpu.make_async_copy(uv_hbm.at[pl.ds(e, 1)], buf.at[slot, pl.ds(j * PEER_K + k, 1)],
                                     sems.at[slot])

    def issue(idx_ref, slot):
        def tok(j, carry):
            for k in range(PEER_K):
                row_copy(idx_ref, slot, j, k).start()
            return carry
        lax.fori_loop(0, PEER_TB, tok, 0)

    @pl.when(i == 0)
    def _():
        issue(idx_cur, 0)

    @pl.when(i + 1 < n)
    def _():
        issue(idx_nxt, (i + 1) % 2)

    slot = i % 2
    pltpu.make_async_copy(uv_hbm.at[pl.ds(0, rows)], buf.at[slot], sems.at[slot]).wait()

    n_tiles = d // LANES
    for j in range(PEER_TB):
        part = None
        for c in range(n_tiles):
            u_c = buf[slot, pl.ds(j * PEER_K, PEER_K), pl.ds(c * LANES, LANES)]
            p = u_c * t_ref[pl.ds(j, 1), pl.ds(c * LANES, LANES)]
            part = p if part is None else part + p
        s_scr[:, pl.ds(j, 1)] = jnp.sum(part, axis=1, keepdims=True)
    s = s_scr[...]
    a_scr[...] = 0.5 * s * (1.0 + lax.erf(s * (2.0 ** -0.5))) * g_ref[0]
    for j in range(PEER_TB):
        a_col = a_scr[:, pl.ds(j, 1)]
        for c in range(n_tiles):
            v_c = buf[slot, pl.ds(j * PEER_K, PEER_K), pl.ds(d + c * LANES, LANES)]
            o_ref[pl.ds(j, 1), pl.ds(c * LANES, LANES)] = jnp.sum(a_col * v_c, axis=0, keepdims=True)


def _peer_experts(t, experts, gates_t, expert_u, expert_v):
    n_tok, d = t.shape
    nb = n_tok // PEER_TB
    uv = jnp.concatenate([expert_u, expert_v], axis=1)
    smem_blk = lambda im: pl.BlockSpec((PEER_TB, PEER_K), im, memory_space=pltpu.SMEM)
    return pl.pallas_call(
        _peer_expert_kernel,
        grid=(nb,),
        in_specs=[smem_blk(lambda i: (i, 0)),
                  smem_blk(lambda i: (jnp.minimum(i + 1, nb - 1), 0)),
                  pl.BlockSpec((PEER_TB, d), lambda i: (i, 0)),
                  pl.BlockSpec((1, PEER_K, PEER_TB), lambda i: (i, 0, 0)),
                  pl.BlockSpec(memory_space=pl.ANY)],
        out_specs=pl.BlockSpec((PEER_TB, d), lambda i: (i, 0)),
        out_shape=jax.ShapeDtypeStruct((n_tok, d), f32),
        scratch_shapes=[pltpu.VMEM((2, PEER_TB * PEER_K, 2 * d), f32),
                        pltpu.SemaphoreType.DMA((2,)),
                        pltpu.VMEM((PEER_K, PEER_TB), f32),
                        pltpu.VMEM((PEER_K, PEER_TB), f32)],
        compiler_params=pltpu.CompilerParams(dimension_semantics=("arbitrary",),
                                             vmem_limit_bytes=40 * 1024 * 1024),
        name="peer_experts",
    )(experts, experts, t, gates_t, uv)


ROUTE_TM = 256
NEG = -jnp.inf
BIG = 2 ** 30


def _top_rows(x, ids, val_ref, id_ref):
    for it in range(P_TOPK):
        m = jnp.max(x, axis=0, keepdims=True)
        sel = jnp.min(jnp.where(x == m, ids, BIG), axis=0, keepdims=True)
        val_ref[it:it + 1, :] = m
        id_ref[it:it + 1, :] = sel
        x = jnp.where(ids == sel, NEG, x)


def _route_kernel(t_ref, wq_ref, keys_ref, exp_ref, gate_ref, v1, i1, v2, i2, vt, it_):
    tm = t_ref.shape[0]
    q = _dot(t_ref[...], wq_ref[...])
    key_ids = lax.broadcasted_iota(jnp.int32, (P_NKEYS, tm), 0)
    sub = lax.broadcasted_iota(jnp.int32, (8, tm), 0)
    for h in range(P_HEADS):
        for c, (vr, ir) in enumerate(((v1, i1), (v2, i2))):
            qb = q[:, (2 * h + c) * P_DQ:(2 * h + c + 1) * P_DQ]
            _top_rows(_dot_nt(keys_ref[h, c], qb), key_ids, vr, ir)
        s1, s2 = v1[...], v2[...]
        cands, cids = [s1[0:1, :] + s2], [lax.broadcasted_iota(jnp.int32, (P_TOPK, tm), 0)]
        for a in range(1, P_TOPK):
            nb = P_TOPK // (a + 1)
            cands.append(jnp.where(sub < nb, s1[a:a + 1, :] + s2[0:8, :], NEG))
            cids.append(sub + a * P_TOPK)
        _top_rows(jnp.concatenate(cands, axis=0), jnp.concatenate(cids, axis=0), vt, it_)
        top_s, top_i = vt[...], it_[...]
        ra = jnp.right_shift(top_i, 4)
        rb = jnp.bitwise_and(top_i, P_TOPK - 1)
        e1 = jnp.zeros_like(top_i)
        e2 = jnp.zeros_like(top_i)
        k1, k2 = i1[...], i2[...]
        for a in range(P_TOPK):
            e1 = jnp.where(ra == a, k1[a:a + 1, :], e1)
            e2 = jnp.where(rb == a, k2[a:a + 1, :], e2)
        exp_ref[h * P_TOPK:(h + 1) * P_TOPK, :] = e1 * P_NKEYS + e2
        e = jnp.exp(top_s - top_s[0:1, :])
        gate_ref[h * P_TOPK:(h + 1) * P_TOPK, :] = e / jnp.sum(e, axis=0, keepdims=True)


def _peer_route(t, w_query, sub_keys):
    n_tok, d = t.shape
    tm = ROUTE_TM
    scr = lambda dt: pltpu.VMEM((P_TOPK, tm), dt)
    return pl.pallas_call(
        _route_kernel,
        grid=(n_tok // tm,),
        in_specs=[pl.BlockSpec((tm, d), lambda i: (i, 0)),
                  pl.BlockSpec(w_query.shape, lambda i: (0, 0)),
                  pl.BlockSpec(sub_keys.shape, lambda i: (0, 0, 0, 0))],
        out_specs=[pl.BlockSpec((PEER_K, tm), lambda i: (0, i)), pl.BlockSpec((PEER_K, tm), lambda i: (0, i))],
        out_shape=[jax.ShapeDtypeStruct((PEER_K, n_tok), jnp.int32), jax.ShapeDtypeStruct((PEER_K, n_tok), f32)],
        scratch_shapes=[scr(f32), scr(jnp.int32), scr(f32), scr(jnp.int32), scr(f32), scr(jnp.int32)],
        compiler_params=pltpu.CompilerParams(dimension_semantics=("parallel",)),
        name="peer_route",
    )(t, w_query.astype(bf16), sub_keys.astype(bf16))


def _peer(h, w_query, sub_keys, expert_u, expert_v):
    b, s, d = h.shape
    n_tok = b * s
    t = h.reshape(n_tok, d)
    experts, gates = _peer_route(t, w_query, sub_keys)
    gates_t = gates.reshape(PEER_K, n_tok // PEER_TB, PEER_TB).transpose(1, 0, 2)
    y = _peer_experts(t, experts.T, gates_t, expert_u, expert_v)
    return y.reshape(b, s, d)


def _final_norm_kernel(x_ref, w_ref, o_ref):
    x = x_ref[...]
    ms = jnp.mean(x * x, axis=-1, keepdims=True)
    o_ref[...] = x * lax.rsqrt(ms + EPS) * w_ref[...]


def _final_norm(x2d, w):
    n, d = x2d.shape
    tm = 512
    return pl.pallas_call(
        _final_norm_kernel,
        grid=(n // tm,),
        in_specs=[pl.BlockSpec((tm, d), lambda i: (i, 0)), pl.BlockSpec((1, d), lambda i: (0, 0))],
        out_specs=pl.BlockSpec((tm, d), lambda i: (i, 0)),
        out_shape=jax.ShapeDtypeStruct((n, d), f32),
        name="final_norm",
    )(x2d, w.reshape(1, d))


def kernel(x, c, ctx, c_ctx, w_ada, b_ada, norm1_w, w_in, conv_w, a_log, dt_bias, gdn_norm_w, lb_logits,
           hg_norm_w, w_pa, w_pb, w_o, norm2_w, w_query, sub_keys, expert_u, expert_v, final_norm_w):
    assert w_ada.shape[0] == 1, "single-layer block"
    lb_all = jnp.cumsum(jax.nn.softmax(lb_logits.astype(f32), axis=0), axis=0)
    l = 0
    mod = jax.nn.silu(c) @ w_ada[l] + b_ada[l]
    mod_c = jax.nn.silu(c_ctx) @ w_ada[l] + b_ada[l]
    sh1, sc1, g1, sh2, sc2, g2 = jnp.split(mod[:, None, :], 6, axis=-1)
    sh1c, sc1c, _, _, _, _ = jnp.split(mod_c, 6, axis=-1)
    h = _rmsnorm(x, norm1_w[l]) * (1 + sc1) + sh1
    hc = _rmsnorm(ctx, norm1_w[l]) * (1 + sc1c) + sh1c
    mix = _token_mixers(h, hc, w_in[l], conv_w[l], a_log[l], dt_bias[l], gdn_norm_w[l],
                        lb_all[l], hg_norm_w[l], w_pa[l], w_pb[l], w_o[l])
    x = x + g1 * mix
    x = x + g2 * _peer(_rmsnorm(x, norm2_w[l]) * (1 + sc2) + sh2,
                       w_query[l], sub_keys[l], expert_u[l], expert_v[l])
    b, s, d = x.shape
    return _final_norm(x.reshape(b * s, d), final_norm_w).reshape(b, s, d)
```

```python
import functools

import jax
import jax.numpy as jnp
from jax import lax
from jax.experimental import pallas as pl
from jax.experimental.pallas import tpu as pltpu

EPS = 1e-6
CHUNK = 64
SEG = 2 * CHUNK
GDN_NC = 8
GRID_W = 64
LANES = 128
HEADS = 8
HEAD_DIM = 128
P_HEADS = 8
P_NKEYS = 128
P_TOPK = 16
P_DQ = 128
PEER_TB = 8
PEER_K = P_HEADS * P_TOPK

HI = lax.Precision.HIGHEST
bf16 = jnp.bfloat16
f32 = jnp.float32


def _dot(a, b):
    return jnp.dot(a.astype(bf16), b.astype(bf16), preferred_element_type=f32)


def _dot_nt(a, b):
    return lax.dot_general(a.astype(bf16), b.astype(bf16), (((1,), (1,)), ((), ())), preferred_element_type=f32)


def _dot_tn(a, b):
    return lax.dot_general(a.astype(bf16), b.astype(bf16), (((0,), (0,)), ((), ())), preferred_element_type=f32)


def _chunk_iota():
    r = lax.broadcasted_iota(jnp.int32, (CHUNK, CHUNK), 0)
    c = lax.broadcasted_iota(jnp.int32, (CHUNK, CHUNK), 1)
    return r, c


def _conv_kernel(x_ref, prev_ref, next_ref, w_ref, o_ref):
    i = pl.program_id(1)
    j = pl.program_id(2)
    x = x_ref[0]
    tm = x.shape[0]
    row = lax.broadcasted_iota(jnp.int32, x.shape, 0)
    prev_row = prev_ref[0, 7:8, :] * (i > 0).astype(f32)
    next_row = next_ref[0, 0:1, :] * (i < pl.num_programs(1) - 1).astype(f32)
    x_prev = jnp.where(row == 0, prev_row, pltpu.roll(x, 1, axis=0))
    x_next = jnp.where(row == tm - 1, next_row, pltpu.roll(x, tm - 1, axis=0))
    y = x_prev * w_ref[0:1, :] + x * w_ref[1:2, :] + x_next * w_ref[2:3, :]
    y = y * jax.nn.sigmoid(y)
    scale = jnp.where(j == 0, HEAD_DIM ** -0.5, 1.0).astype(f32)
    is_v = j == 2
    for h in range(HEADS):
        seg = y[:, h * HEAD_DIM:(h + 1) * HEAD_DIM]
        nrm = seg * lax.rsqrt(jnp.sum(seg * seg, axis=1, keepdims=True) + EPS) * scale
        o_ref[0, :, h * HEAD_DIM:(h + 1) * HEAD_DIM] = jnp.where(is_v, seg, nrm)


def _qkv_conv(p3, conv_w, tm):
    bsz, seq, _ = p3.shape
    width = conv_w.shape[1] // 3
    r8 = tm // 8
    return pl.pallas_call(
        _conv_kernel,
        grid=(bsz, seq // tm, 3),
        in_specs=[pl.BlockSpec((1, tm, width), lambda b, i, j: (b, i, j)),
                  pl.BlockSpec((1, 8, width), lambda b, i, j: (b, jnp.maximum(i * r8 - 1, 0), j)),
                  pl.BlockSpec((1, 8, width), lambda b, i, j: (b, jnp.minimum((i + 1) * r8, seq // 8 - 1), j)),
                  pl.BlockSpec((3, width), lambda b, i, j: (0, j))],
        out_specs=pl.BlockSpec((1, tm, width), lambda b, i, j: (b, i, j)),
        out_shape=jax.ShapeDtypeStruct((bsz, seq, 3 * width), f32),
        compiler_params=pltpu.CompilerParams(dimension_semantics=("parallel", "parallel", "parallel")),
        name="qkv_conv",
    )(p3, p3, p3, conv_w)


def _tri_inverse_many(ms):
    r, c = _chunk_iota()
    eye = (r == c).astype(f32)
    blk = (r // 8) == (c // 8)
    md = [jnp.where(blk, m, 0.0) for m in ms]
    md2 = [_dot(a, a) for a in md]
    md4 = [_dot(a, a) for a in md2]
    x = [_dot(eye - a, eye + b) for a, b in zip(md, md2)]
    x = [_dot(a, eye + b) for a, b in zip(x, md4)]
    b = 8
    while b < CHUNK:
        sel = ((r // (2 * b)) == (c // (2 * b))) & ((r // b) != (c // b))
        xl = [_dot(a, jnp.where(sel, m, 0.0)) for a, m in zip(x, ms)]
        x = [a - _dot(t, a) for a, t in zip(x, xl)]
        b *= 2
    return x


def _softplus(x):
    return jnp.maximum(x, 0.0) + jnp.log(1.0 + jnp.exp(-jnp.abs(x)))


def _gdn_kernel(alog_ref, dtb_ref, qf, kf, vf, abf, qb, kb, vb, abb, s0f, s0b,
                of_ref, ob_ref, sf_out, sb_out, sf, sb, *, nc):
    seg = nc * CHUNK
    h = pl.program_id(1)
    step = pl.program_id(2)

    @pl.when(step == 0)
    def _():
        sf[...] = s0f[0, 0]
        sb[...] = s0b[0, 0]

    jr = lax.broadcasted_iota(jnp.int32, (seg, seg), 0)
    ic = lax.broadcasted_iota(jnp.int32, (seg, seg), 1)
    same = (jr // CHUNK) == (ic // CHUNK)
    r, c = _chunk_iota()

    def prep(ab_ref, d):
        ab = ab_ref[0, 0]
        g = -jnp.exp(alog_ref[d, h]) * _softplus(ab[d:d + 1, :] + dtb_ref[d, h])
        beta = jax.nn.sigmoid(ab[2 + d:3 + d, :])
        cum = (same & ((jr >= ic) if d else (jr <= ic))).astype(f32)
        rows = jnp.concatenate([g, jnp.zeros((7, seg), f32)], axis=0)
        gc = jnp.dot(rows, cum, precision=HI, preferred_element_type=f32)[0:1, :]
        cols = []
        for p in range(seg // LANES):
            sl = slice(p * LANES, (p + 1) * LANES)
            cols.append(jnp.concatenate([gc[:, sl], beta[:, sl], jnp.zeros((LANES - 2, LANES), f32)], axis=0).T)
        cols = jnp.concatenate(cols, axis=0)
        return gc, cols[:, 0:1], cols[:, 1:2]

    inst = []
    for d, (q_ref, k_ref, v_ref, ab_ref) in enumerate(((qf, kf, vf, abf), (qb, kb, vb, abb))):
        gcr, gcc, bcol = prep(ab_ref, d)
        for ci in range(nc):
            sl = slice(ci * CHUNK, (ci + 1) * CHUNK)
            gl = gcr[:, ci * CHUNK:ci * CHUNK + 1] if d else gcr[:, (ci + 1) * CHUNK - 1:(ci + 1) * CHUNK]
            inst.append(dict(d=d, sl=sl, q=q_ref[0, sl, :], k=k_ref[0, sl, :], v=v_ref[0, sl, :],
                             gcc=gcc[sl], gcr=gcr[:, sl], b=bcol[sl], gl=gl))
    for t in inst:
        incl = (r <= c) if t['d'] else (r >= c)
        t['strict'] = (r < c) if t['d'] else (r > c)
        t['decay'] = jnp.where(incl, jnp.exp(jnp.where(incl, t['gcc'] - t['gcr'], 0.0)), 0.0)
        t['kb'] = t['k'] * t['b']
        t['eg'] = jnp.exp(t['gcc'])
    kk = [_dot_nt(t['kb'], t['k']) for t in inst]
    qk = [_dot_nt(t['q'], t['k']) for t in inst]
    tinv = _tri_inverse_many([jnp.where(t['strict'], a * t['decay'], 0.0) for t, a in zip(inst, kk)])
    sol = [_dot(x, jnp.concatenate([t['v'] * t['b'], t['kb'] * t['eg']], axis=1)) for t, x in zip(inst, tinv)]
    for t, s_, a in zip(inst, sol, qk):
        t['u'], t['w'] = s_[:, :HEAD_DIM], s_[:, HEAD_DIM:]
        t['qk'] = a * t['decay']
        t['qd'] = t['q'] * t['eg']
        t['kd'] = t['k'] * jnp.exp(t['gl'] - t['gcc'])
        t['tot'] = jnp.exp(t['gl'])

    st = [sf[...], sb[...]]
    fwd = [t for t in inst if t['d'] == 0]
    bwd = [t for t in inst if t['d'] == 1][::-1]
    for tf, tb in zip(fwd, bwd):
        for t, o_ref in ((tf, of_ref), (tb, ob_ref)):
            s_ = st[t['d']]
            v_new = t['u'] - _dot(t['w'], s_)
            o_ref[0, t['sl'], :] = _dot(t['qd'], s_) + _dot(t['qk'], v_new)
            st[t['d']] = t['tot'] * s_ + _dot_tn(t['kd'], v_new)
    sf[...] = st[0]
    sb[...] = st[1]

    @pl.when(step == pl.num_programs(2) - 1)
    def _():
        sf_out[0, 0] = st[0]
        sb_out[0, 0] = st[1]


def _gdn_scan(qkv, abt, a_log, dt_bias, s0f, s0b, nc):
    bsz, seq, _ = qkv.shape
    seg = nc * CHUNK
    n_steps = seq // seg
    pos = lambda s, rev: (n_steps - 1 - s) if rev else s
    in_spec = lambda cb, rev: pl.BlockSpec((1, seg, HEAD_DIM), lambda b, h, s: (b, pos(s, rev), cb + h))
    ab_spec = lambda rev: pl.BlockSpec((1, 1, 4, seg), lambda b, h, s: (b, h, 0, pos(s, rev)))
    out_spec = lambda rev: pl.BlockSpec((1, seg, HEAD_DIM), lambda b, h, s: (b, pos(s, rev), h))
    smem = pl.BlockSpec(memory_space=pltpu.SMEM)
    st_spec = pl.BlockSpec((1, 1, HEAD_DIM, HEAD_DIM), lambda b, h, s: (b, h, 0, 0))
    o_shape = jax.ShapeDtypeStruct((bsz, seq, HEADS * HEAD_DIM), f32)
    st_shape = jax.ShapeDtypeStruct((bsz, HEADS, HEAD_DIM, HEAD_DIM), f32)
    return pl.pallas_call(
        functools.partial(_gdn_kernel, nc=nc),
        grid=(bsz, HEADS, n_steps),
        in_specs=[smem, smem,
                  in_spec(0, False), in_spec(HEADS, False), in_spec(2 * HEADS, False), ab_spec(False),
                  in_spec(0, True), in_spec(HEADS, True), in_spec(2 * HEADS, True), ab_spec(True),
                  st_spec, st_spec],
        out_specs=[out_spec(False), out_spec(True), st_spec, st_spec],
        out_shape=[o_shape, o_shape, st_shape, st_shape],
        scratch_shapes=[pltpu.VMEM((HEAD_DIM, HEAD_DIM), f32), pltpu.VMEM((HEAD_DIM, HEAD_DIM), f32)],
        compiler_params=pltpu.CompilerParams(dimension_semantics=("parallel", "parallel", "arbitrary")),
        name="gdn_scan",
    )(a_log, dt_bias, qkv, qkv, qkv, abt, qkv, qkv, qkv, abt, s0f, s0b)


def _gla_chunk(q_raw, f_raw, v, lb, st, reverse):
    r, c = _chunk_iota()
    mask = (r <= c) if reverse else (r >= c)
    q = q_raw * jax.nn.sigmoid(q_raw) * (HEAD_DIM ** -0.5)
    f = lb + (1.0 - lb) * jax.nn.sigmoid(f_raw)
    logf = jnp.log(f)
    k = 1.0 - f
    bc = jnp.dot(mask.astype(f32), logf, precision=HI, preferred_element_type=f32)
    ref_i = CHUNK // 2 - 1 if reverse else CHUNK // 2
    last_i = 0 if reverse else CHUNK - 1
    ref = bc[ref_i:ref_i + 1, :]
    last = bc[last_i:last_i + 1, :]
    sc = jnp.where(mask, _dot_nt(q * jnp.exp(bc - ref), k * jnp.exp(ref - bc)), 0.0)
    o = _dot(sc, v) + _dot_nt(q * jnp.exp(bc), st)
    st = st * jnp.exp(last) + _dot_tn(v, k * jnp.exp(last - bc))
    return o, st


def _gla_kernel(qf, ff, vf, qb, fb, vb, lb_ref, s0f, s0b, of_ref, ob_ref, sf_out, sb_out, sf, sb):
    step = pl.program_id(2)

    @pl.when(step == 0)
    def _():
        sf[...] = s0f[0, 0]
        sb[...] = s0b[0, 0]

    lb = lb_ref[...]
    st_f = sf[...]
    st_b = sb[...]
    for half in range(SEG // CHUNK):
        sl = slice(half * CHUNK, (half + 1) * CHUNK)
        o, st_f = _gla_chunk(qf[0, sl, :], ff[0, sl, :], vf[0, sl, :], lb, st_f, False)
        of_ref[0, sl, :] = o
        sl = slice(SEG - (half + 1) * CHUNK, SEG - half * CHUNK)
        o, st_b = _gla_chunk(qb[0, sl, :], fb[0, sl, :], vb[0, sl, :], lb, st_b, True)
        ob_ref[0, sl, :] = o
    sf[...] = st_f
    sb[...] = st_b

    @pl.when(step == pl.num_programs(2) - 1)
    def _():
        sf_out[0, 0] = st_f
        sb_out[0, 0] = st_b


def _gla_scan(p3, col_blocks, lb, s0f, s0b, n_steps, column_major):
    bsz = p3.shape[0]
    cq, cff, cfb, cv = col_blocks
    pos = lambda s, rev: (n_steps - 1 - s) if rev else s
    if column_major:
        lane_blocks = p3.shape[2] // (n_steps * LANES)
        in_spec = lambda cb, rev: pl.BlockSpec((1, SEG, LANES), lambda b, h, s: (b, 0, pos(s, rev) * lane_blocks + cb + h))
        out_spec = lambda rev: pl.BlockSpec((1, SEG, LANES), lambda b, h, s: (b, 0, pos(s, rev) * HEADS + h))
        o_shape = jax.ShapeDtypeStruct((bsz, SEG, n_steps * HEADS * HEAD_DIM), f32)
    else:
        in_spec = lambda cb, rev: pl.BlockSpec((1, SEG, LANES), lambda b, h, s: (b, pos(s, rev), cb + h))
        out_spec = lambda rev: pl.BlockSpec((1, SEG, LANES), lambda b, h, s: (b, pos(s, rev), h))
        o_shape = jax.ShapeDtypeStruct((bsz, n_steps * SEG, HEADS * HEAD_DIM), f32)
    st_spec = pl.BlockSpec((1, 1, HEAD_DIM, HEAD_DIM), lambda b, h, s: (b, h, 0, 0))
    st_shape = jax.ShapeDtypeStruct((bsz, HEADS, HEAD_DIM, HEAD_DIM), f32)
    return pl.pallas_call(
        _gla_kernel,
        grid=(bsz, HEADS, n_steps),
        in_specs=[in_spec(cq, False), in_spec(cff, False), in_spec(cv, False),
                  in_spec(cq, True), in_spec(cfb, True), in_spec(cv, True),
                  pl.BlockSpec((1, HEAD_DIM), lambda b, h, s: (0, h)), st_spec, st_spec],
        out_specs=[out_spec(False), out_spec(True), st_spec, st_spec],
        out_shape=[o_shape, o_shape, st_shape, st_shape],
        scratch_shapes=[pltpu.VMEM((HEAD_DIM, HEAD_DIM), f32), pltpu.VMEM((HEAD_DIM, HEAD_DIM), f32)],
        compiler_params=pltpu.CompilerParams(dimension_semantics=("parallel", "parallel", "arbitrary")),
        name="hgrn2_scan",
    )(p3, p3, p3, p3, p3, p3, lb.reshape(1, -1), s0f, s0b)


_W = HEADS * HEAD_DIM
C_QKV, C_GA, C_QB, C_FF, C_FB, C_IB, C_GB, C_MG, C_AB = (0, 3 * _W, 4 * _W, 5 * _W, 6 * _W, 7 * _W, 8 * _W, 9 * _W, 11 * _W)
N_MAIN = 11 * _W
PROJ_TM = 512
MERGE_TM = 256


def _modulated_norm(x, w, shift, scale):
    return x * lax.rsqrt(jnp.mean(x * x, axis=-1, keepdims=True) + EPS) * w * (1.0 + scale) + shift


def _in_proj_kernel(x_ref, nw_ref, sh_ref, sc_ref, w_ref, wab_ref, p_ref, ab_ref, h_scr):
    j = pl.program_id(1)

    @pl.when(j == 0)
    def _():
        h = _modulated_norm(x_ref[...], nw_ref[...], sh_ref[0], sc_ref[0]).astype(bf16)
        h_scr[...] = h
        ab_ref[...] = jnp.dot(h, wab_ref[...], preferred_element_type=f32)

    p_ref[...] = jnp.dot(h_scr[...], w_ref[...], preferred_element_type=f32)


def _in_proj(x2, norm_w, shift, scale, w_main, w_ab):
    rows, d = x2.shape
    tm = min(PROJ_TM, rows // shift.shape[0])
    per_mod = rows // shift.shape[0] // tm
    mod_spec = pl.BlockSpec((1, 1, d), lambda i, j: (i // per_mod, 0, 0))
    return pl.pallas_call(
        _in_proj_kernel,
        grid=(rows // tm, N_MAIN // _W),
        in_specs=[pl.BlockSpec((tm, d), lambda i, j: (i, 0)),
                  pl.BlockSpec((1, d), lambda i, j: (0, 0)), mod_spec, mod_spec,
                  pl.BlockSpec((d, _W), lambda i, j: (0, j)),
                  pl.BlockSpec((d, LANES), lambda i, j: (0, 0))],
        out_specs=[pl.BlockSpec((tm, _W), lambda i, j: (i, j)), pl.BlockSpec((tm, LANES), lambda i, j: (i, 0))],
        out_shape=[jax.ShapeDtypeStruct((rows, N_MAIN), f32), jax.ShapeDtypeStruct((rows, LANES), f32)],
        scratch_shapes=[pltpu.VMEM((tm, d), bf16)],
        compiler_params=pltpu.CompilerParams(dimension_semantics=("parallel", "arbitrary")),
        name="in_proj",
    )(x2, norm_w.reshape(1, d), shift[:, None, :], scale[:, None, :], w_main, w_ab)


def _split_w_in(w_in):
    ab0 = 4 * _W
    ab1 = ab0 + 4 * HEADS
    pad = jnp.zeros((w_in.shape[0], LANES - 4 * HEADS), w_in.dtype)
    w_main = jnp.concatenate([w_in[:, :ab0], w_in[:, ab1:]], axis=1)
    w_ab = jnp.concatenate([w_in[:, ab0:ab1], pad], axis=1)
    return w_main.astype(bf16), w_ab.astype(bf16)


def _mixer_scans(p, ab, p_ctx, ab_ctx, conv_w, a_log, dt_bias, lb):
    bsz, seq, _ = p.shape
    ctx_len = p_ctx.shape[1]
    rows = seq // GRID_W
    zeros = jnp.zeros((bsz, HEADS, HEAD_DIM, HEAD_DIM), f32)
    abt = lambda t: t[:, :, :4 * HEADS].reshape(bsz, -1, 4, HEADS).transpose(0, 3, 2, 1)
    _, _, sa_f, sa_b = _gdn_scan(_qkv_conv(p_ctx, conv_w, ctx_len), abt(ab_ctx), a_log, dt_bias, zeros, zeros,
                                 min(GDN_NC, ctx_len // CHUNK))
    oa_f, oa_b, _, _ = _gdn_scan(_qkv_conv(p, conv_w, 512), abt(ab), a_log, dt_bias, sa_f, sa_b, GDN_NC)
    cols = tuple(c // LANES for c in (C_QB, C_FF, C_FB, C_IB))
    _, _, sb_f, sb_b = _gla_scan(p_ctx, cols, lb, zeros, zeros, ctx_len // SEG, False)
    ob_f, ob_b, _, _ = _gla_scan(p.reshape(bsz, rows, GRID_W * N_MAIN), cols, lb, sb_f, sb_b, GRID_W, True)
    flat = lambda t: t.reshape(bsz * seq, _W)
    return flat(oa_f), flat(oa_b), flat(ob_f), flat(ob_b)


def _merge_kernel(oaf, oab, obf, obb, ga, gb, mga, mgb, x_ref, g1_ref, sh2_ref, sc2_ref, na_ref, nb_ref, n2_ref,
                  wpa, wpb, wo, x1_ref, t_ref):
    def head_out(o, nw_ref, gate):
        outs = []
        for h in range(HEADS):
            seg = o[:, h * HEAD_DIM:(h + 1) * HEAD_DIM]
            outs.append(seg * lax.rsqrt(jnp.mean(seg * seg, axis=1, keepdims=True) + EPS) * nw_ref[...])
        return jnp.concatenate(outs, axis=1) * (gate * jax.nn.sigmoid(gate))

    o_a = head_out(oaf[...] + oab[...], na_ref, ga[...])
    o_b = head_out(obf[...] + obb[...], nb_ref, gb[...])
    merged = jax.nn.sigmoid(mga[...]) * _dot(o_a, wpa[...]) + jax.nn.sigmoid(mgb[...]) * _dot(o_b, wpb[...])
    x1 = x_ref[...] + g1_ref[0] * _dot(merged, wo[...])
    x1_ref[...] = x1
    t_ref[...] = _modulated_norm(x1, n2_ref[...], sh2_ref[0], sc2_ref[0])


def _merge(scans, p2, x2, g1, sh2, sc2, gdn_norm_w, hg_norm_w, norm2_w, w_pa, w_pb, w_o):
    n_tok, d = x2.shape
    tm = MERGE_TM
    per_mod = n_tok // g1.shape[0] // tm
    row = lambda cb: pl.BlockSpec((tm, d), lambda i: (i, cb))
    mod = pl.BlockSpec((1, 1, d), lambda i: (i // per_mod, 0, 0))
    vec = lambda n: pl.BlockSpec((1, n), lambda i: (0, 0))
    mat = pl.BlockSpec((d, d), lambda i: (0, 0))
    return pl.pallas_call(
        _merge_kernel,
        grid=(n_tok // tm,),
        in_specs=[row(0), row(0), row(0), row(0),
                  row(C_GA // d), row(C_GB // d), row(C_MG // d), row(C_MG // d + 1),
                  row(0), mod, mod, mod, vec(HEAD_DIM), vec(HEAD_DIM), vec(d), mat, mat, mat],
        out_specs=[row(0), row(0)],
        out_shape=[jax.ShapeDtypeStruct((n_tok, d), f32), jax.ShapeDtypeStruct((n_tok, d), f32)],
        compiler_params=pltpu.CompilerParams(dimension_semantics=("parallel",), vmem_limit_bytes=48 * 1024 * 1024),
        name="merge",
    )(*scans, p2, p2, p2, p2, x2, g1[:, None, :], sh2[:, None, :], sc2[:, None, :], gdn_norm_w.reshape(1, -1), hg_norm_w.reshape(1, -1),
      norm2_w.reshape(1, -1), w_pa.astype(bf16), w_pb.astype(bf16), w_o.astype(bf16))


def _peer_expert_kernel(idx_cur, idx_nxt, t_ref, g_ref, uv_hbm, o_ref, buf, sems, s_scr, a_scr):
    i = pl.program_id(0)
    n = pl.num_programs(0)
    d = t_ref.shape[1]
    rows = PEER_TB * PEER_K

    def row_copy(idx_ref, slot, j, k):
        e = idx_ref[j, k]
        return pltpu.make_async_copy(uv_hbm.at[pl.ds(e, 1)], buf.at[slot, pl.ds(j * PEER_K + k, 1)],
                                     sems.at[slot])

    def issue(idx_ref, slot):
        def tok(j, carry):
            for k in range(PEER_K):
                row_copy(idx_ref, slot, j, k).start()
            return carry
        lax.fori_loop(0, PEER_TB, tok, 0)

    @pl.when(i == 0)
    def _():
        issue(idx_cur, 0)

    @pl.when(i + 1 < n)
    def _():
        issue(idx_nxt, (i + 1) % 2)

    slot = i % 2
    pltpu.make_async_copy(uv_hbm.at[pl.ds(0, rows)], buf.at[slot], sems.at[slot]).wait()

    n_tiles = d // LANES
    for j in range(PEER_TB):
        part = None
        for c in range(n_tiles):
            u_c = buf[slot, pl.ds(j * PEER_K, PEER_K), pl.ds(c * LANES, LANES)]
            p = u_c * t_ref[pl.ds(j, 1), pl.ds(c * LANES, LANES)]
            part = p if part is None else part + p
        s_scr[:, pl.ds(j, 1)] = jnp.sum(part, axis=1, keepdims=True)
    s = s_scr[...]
    a_scr[...] = 0.5 * s * (1.0 + lax.erf(s * (2.0 ** -0.5))) * g_ref[0]
    for j in range(PEER_TB):
        a_col = a_scr[:, pl.ds(j, 1)]
        for c in range(n_tiles):
            v_c = buf[slot, pl.ds(j * PEER_K, PEER_K), pl.ds(d + c * LANES, LANES)]
            o_ref[pl.ds(j, 1), pl.ds(c * LANES, LANES)] = jnp.sum(a_col * v_c, axis=0, keepdims=True)


def _peer_experts(t, experts, gates_t, expert_u, expert_v):
    n_tok, d = t.shape
    nb = n_tok // PEER_TB
    uv = jnp.concatenate([expert_u, expert_v], axis=1)
    smem_blk = lambda im: pl.BlockSpec((PEER_TB, PEER_K), im, memory_space=pltpu.SMEM)
    return pl.pallas_call(
        _peer_expert_kernel,
        grid=(nb,),
        in_specs=[smem_blk(lambda i: (i, 0)),
                  smem_blk(lambda i: (jnp.minimum(i + 1, nb - 1), 0)),
                  pl.BlockSpec((PEER_TB, d), lambda i: (i, 0)),
                  pl.BlockSpec((1, PEER_K, PEER_TB), lambda i: (i, 0, 0)),
                  pl.BlockSpec(memory_space=pl.ANY)],
        out_specs=pl.BlockSpec((PEER_TB, d), lambda i: (i, 0)),
        out_shape=jax.ShapeDtypeStruct((n_tok, d), f32),
        scratch_shapes=[pltpu.VMEM((2, PEER_TB * PEER_K, 2 * d), f32),
                        pltpu.SemaphoreType.DMA((2,)),
                        pltpu.VMEM((PEER_K, PEER_TB), f32),
                        pltpu.VMEM((PEER_K, PEER_TB), f32)],
        compiler_params=pltpu.CompilerParams(dimension_semantics=("arbitrary",),
                                             vmem_limit_bytes=40 * 1024 * 1024),
        name="peer_experts",
    )(experts, experts, t, gates_t, uv)


ROUTE_TM = 256
NEG = -jnp.inf
BIG = 2 ** 30


def _top_rows(x, ids, val_ref, id_ref):
    for it in range(P_TOPK):
        m = jnp.max(x, axis=0, keepdims=True)
        sel = jnp.min(jnp.where(x == m, ids, BIG), axis=0, keepdims=True)
        val_ref[it:it + 1, :] = m
        id_ref[it:it + 1, :] = sel
        x = jnp.where(ids == sel, NEG, x)


def _route_kernel(t_ref, wq_ref, keys_ref, exp_ref, gate_ref, v1, i1, v2, i2, vt, it_):
    tm = t_ref.shape[0]
    q = _dot(t_ref[...], wq_ref[...])
    key_ids = lax.broadcasted_iota(jnp.int32, (P_NKEYS, tm), 0)
    sub = lax.broadcasted_iota(jnp.int32, (8, tm), 0)
    for h in range(P_HEADS):
        for c, (vr, ir) in enumerate(((v1, i1), (v2, i2))):
            qb = q[:, (2 * h + c) * P_DQ:(2 * h + c + 1) * P_DQ]
            _top_rows(_dot_nt(keys_ref[h, c], qb), key_ids, vr, ir)
        s1, s2 = v1[...], v2[...]
        cands, cids = [s1[0:1, :] + s2], [lax.broadcasted_iota(jnp.int32, (P_TOPK, tm), 0)]
        for a in range(1, P_TOPK):
            nb = P_TOPK // (a + 1)
            cands.append(jnp.where(sub < nb, s1[a:a + 1, :] + s2[0:8, :], NEG))
            cids.append(sub + a * P_TOPK)
        _top_rows(jnp.concatenate(cands, axis=0), jnp.concatenate(cids, axis=0), vt, it_)
        top_s, top_i = vt[...], it_[...]
        ra = jnp.right_shift(top_i, 4)
        rb = jnp.bitwise_and(top_i, P_TOPK - 1)
        e1 = jnp.zeros_like(top_i)
        e2 = jnp.zeros_like(top_i)
        k1, k2 = i1[...], i2[...]
        for a in range(P_TOPK):
            e1 = jnp.where(ra == a, k1[a:a + 1, :], e1)
            e2 = jnp.where(rb == a, k2[a:a + 1, :], e2)
        exp_ref[h * P_TOPK:(h + 1) * P_TOPK, :] = e1 * P_NKEYS + e2
        e = jnp.exp(top_s - top_s[0:1, :])
        gate_ref[h * P_TOPK:(h + 1) * P_TOPK, :] = e / jnp.sum(e, axis=0, keepdims=True)


def _peer_route(t, w_query, sub_keys):
    n_tok, d = t.shape
    tm = ROUTE_TM
    scr = lambda dt: pltpu.VMEM((P_TOPK, tm), dt)
    return pl.pallas_call(
        _route_kernel,
        grid=(n_tok // tm,),
        in_specs=[pl.BlockSpec((tm, d), lambda i: (i, 0)),
                  pl.BlockSpec(w_query.shape, lambda i: (0, 0)),
                  pl.BlockSpec(sub_keys.shape, lambda i: (0, 0, 0, 0))],
        out_specs=[pl.BlockSpec((PEER_K, tm), lambda i: (0, i)), pl.BlockSpec((PEER_K, tm), lambda i: (0, i))],
        out_shape=[jax.ShapeDtypeStruct((PEER_K, n_tok), jnp.int32), jax.ShapeDtypeStruct((PEER_K, n_tok), f32)],
        scratch_shapes=[scr(f32), scr(jnp.int32), scr(f32), scr(jnp.int32), scr(f32), scr(jnp.int32)],
        compiler_params=pltpu.CompilerParams(dimension_semantics=("parallel",)),
        name="peer_route",
    )(t, w_query.astype(bf16), sub_keys.astype(bf16))


def _peer(t, w_query, sub_keys, expert_u, expert_v):
    n_tok = t.shape[0]
    experts, gates = _peer_route(t, w_query, sub_keys)
    gates_t = gates.reshape(PEER_K, n_tok // PEER_TB, PEER_TB).transpose(1, 0, 2)
    return _peer_experts(t, experts.T, gates_t, expert_u, expert_v)


def _final_norm_kernel(x_ref, y_ref, g_ref, w_ref, o_ref):
    x = x_ref[...] + g_ref[0] * y_ref[...]
    o_ref[...] = x * lax.rsqrt(jnp.mean(x * x, axis=-1, keepdims=True) + EPS) * w_ref[...]


def _final_norm(x1, y, g2, w):
    n, d = x1.shape
    tm = PROJ_TM
    per_mod = n // g2.shape[0] // tm
    row = pl.BlockSpec((tm, d), lambda i: (i, 0))
    return pl.pallas_call(
        _final_norm_kernel,
        grid=(n // tm,),
        in_specs=[row, row, pl.BlockSpec((1, 1, d), lambda i: (i // per_mod, 0, 0)),
                  pl.BlockSpec((1, d), lambda i: (0, 0))],
        out_specs=row,
        out_shape=jax.ShapeDtypeStruct((n, d), f32),
        compiler_params=pltpu.CompilerParams(dimension_semantics=("parallel",)),
        name="final_norm",
    )(x1, y, g2[:, None, :], w.reshape(1, d))


def kernel(x, c, ctx, c_ctx, w_ada, b_ada, norm1_w, w_in, conv_w, a_log, dt_bias, gdn_norm_w, lb_logits,
           hg_norm_w, w_pa, w_pb, w_o, norm2_w, w_query, sub_keys, expert_u, expert_v, final_norm_w):
    assert w_ada.shape[0] == 1, "single-layer block"
    bsz, seq, d = x.shape
    assert seq // GRID_W == SEG, "the column-major scan reads one grid column (SEG rows) per step"
    l = 0
    lb = jnp.cumsum(jax.nn.softmax(lb_logits.astype(f32), axis=0), axis=0)[l]
    mod = jax.nn.silu(c) @ w_ada[l] + b_ada[l]
    mod_c = (jax.nn.silu(c_ctx) @ w_ada[l] + b_ada[l])[None, :]
    sh1, sc1, g1, sh2, sc2, g2 = jnp.split(mod, 6, axis=-1)
    sh1c, sc1c = mod_c[:, :d], mod_c[:, d:2 * d]
    w_main, w_ab = _split_w_in(w_in[l])
    x2 = x.reshape(bsz * seq, d)
    p, ab = _in_proj(x2, norm1_w[l], sh1, sc1, w_main, w_ab)
    p_ctx, ab_ctx = _in_proj(ctx.reshape(-1, d), norm1_w[l], sh1c, sc1c, w_main, w_ab)
    three_d = lambda t, n: t.reshape(bsz, -1, n)
    scans = _mixer_scans(three_d(p, N_MAIN), three_d(ab, LANES), three_d(p_ctx, N_MAIN), three_d(ab_ctx, LANES),
                         conv_w[l], a_log[l], dt_bias[l], lb)
    x1, t = _merge(scans, p, x2, g1, sh2, sc2, gdn_norm_w[l], hg_norm_w[l], norm2_w[l], w_pa[l], w_pb[l], w_o[l])
    y = _peer(t, w_query[l], sub_keys[l], expert_u[l], expert_v[l])
    return _final_norm(x1, y, g2, final_norm_w).reshape(bsz, seq, d)
```

```python
import functools

import jax
import jax.numpy as jnp
import numpy as np
from jax import lax
from jax.experimental import pallas as pl
from jax.experimental.pallas import tpu as pltpu

EPS = 1e-6
CHUNK = 64
SEG = 2 * CHUNK
GDN_NC = 8
GRID_W = 64
LANES = 128
SUBLANES = 8
HEADS = 8
HEAD_DIM = 128
P_HEADS = 8
P_NKEYS = 128
P_TOPK = 16
P_DQ = 128
PEER_TB = 8
PEER_K = P_HEADS * P_TOPK

HI = lax.Precision.HIGHEST
bf16 = jnp.bfloat16
f32 = jnp.float32


def _dot(a, b):
    return jnp.dot(a.astype(bf16), b.astype(bf16), preferred_element_type=f32)


def _dot_nt(a, b):
    return lax.dot_general(a.astype(bf16), b.astype(bf16), (((1,), (1,)), ((), ())), preferred_element_type=f32)


def _dot_tn(a, b):
    return lax.dot_general(a.astype(bf16), b.astype(bf16), (((0,), (0,)), ((), ())), preferred_element_type=f32)


def _chunk_iota():
    r = lax.broadcasted_iota(jnp.int32, (CHUNK, CHUNK), 0)
    c = lax.broadcasted_iota(jnp.int32, (CHUNK, CHUNK), 1)
    return r, c


def _conv_kernel(x_ref, prev_ref, next_ref, w_ref, o_ref):
    i = pl.program_id(1)
    j = pl.program_id(2)
    x = x_ref[0]
    tm = x.shape[0]
    row = lax.broadcasted_iota(jnp.int32, x.shape, 0)
    prev_row = prev_ref[0, 7:8, :] * (i > 0).astype(f32)
    next_row = next_ref[0, 0:1, :] * (i < pl.num_programs(1) - 1).astype(f32)
    x_prev = jnp.where(row == 0, prev_row, pltpu.roll(x, 1, axis=0))
    x_next = jnp.where(row == tm - 1, next_row, pltpu.roll(x, tm - 1, axis=0))
    y = x_prev * w_ref[0:1, :] + x * w_ref[1:2, :] + x_next * w_ref[2:3, :]
    y = y * jax.nn.sigmoid(y)
    scale = jnp.where(j == 0, HEAD_DIM ** -0.5, 1.0).astype(f32)
    is_v = j == 2
    for h in range(HEADS):
        seg = y[:, h * HEAD_DIM:(h + 1) * HEAD_DIM]
        nrm = seg * lax.rsqrt(jnp.sum(seg * seg, axis=1, keepdims=True) + EPS) * scale
        o_ref[0, :, h * HEAD_DIM:(h + 1) * HEAD_DIM] = jnp.where(is_v, seg, nrm)


def _qkv_conv(p3, conv_w, tm):
    bsz, seq, _ = p3.shape
    width = conv_w.shape[1] // 3
    r8 = tm // 8
    return pl.pallas_call(
        _conv_kernel,
        grid=(bsz, seq // tm, 3),
        in_specs=[pl.BlockSpec((1, tm, width), lambda b, i, j: (b, i, j)),
                  pl.BlockSpec((1, 8, width), lambda b, i, j: (b, jnp.maximum(i * r8 - 1, 0), j)),
                  pl.BlockSpec((1, 8, width), lambda b, i, j: (b, jnp.minimum((i + 1) * r8, seq // 8 - 1), j)),
                  pl.BlockSpec((3, width), lambda b, i, j: (0, j))],
        out_specs=pl.BlockSpec((1, tm, width), lambda b, i, j: (b, i, j)),
        out_shape=jax.ShapeDtypeStruct((bsz, seq, 3 * width), f32),
        compiler_params=pltpu.CompilerParams(dimension_semantics=("parallel", "parallel", "parallel")),
        name="qkv_conv",
    )(p3, p3, p3, conv_w)


def _tri_inverse_many(ms):
    r, c = _chunk_iota()
    eye = (r == c).astype(f32)
    blk = (r // 8) == (c // 8)
    md = [jnp.where(blk, m, 0.0) for m in ms]
    md2 = [_dot(a, a) for a in md]
    md4 = [_dot(a, a) for a in md2]
    x = [_dot(eye - a, eye + b) for a, b in zip(md, md2)]
    x = [_dot(a, eye + b) for a, b in zip(x, md4)]
    b = 8
    while b < CHUNK:
        sel = ((r // (2 * b)) == (c // (2 * b))) & ((r // b) != (c // b))
        xl = [_dot(a, jnp.where(sel, m, 0.0)) for a, m in zip(x, ms)]
        x = [a - _dot(t, a) for a, t in zip(x, xl)]
        b *= 2
    return x


def _softplus(x):
    return jnp.maximum(x, 0.0) + jnp.log(1.0 + jnp.exp(-jnp.abs(x)))


def _gdn_kernel(alog_ref, dtb_ref, qf, kf, vf, abf, qb, kb, vb, abb, s0f, s0b,
                of_ref, ob_ref, sf_out, sb_out, sf, sb, *, nc):
    seg = nc * CHUNK
    h = pl.program_id(1)
    step = pl.program_id(2)

    @pl.when(step == 0)
    def _():
        sf[...] = s0f[0, 0]
        sb[...] = s0b[0, 0]

    jr = lax.broadcasted_iota(jnp.int32, (seg, seg), 0)
    ic = lax.broadcasted_iota(jnp.int32, (seg, seg), 1)
    same = (jr // CHUNK) == (ic // CHUNK)
    r, c = _chunk_iota()

    def prep(ab_ref, d):
        ab = ab_ref[0, 0]
        g = -jnp.exp(alog_ref[d, h]) * _softplus(ab[d:d + 1, :] + dtb_ref[d, h])
        beta = jax.nn.sigmoid(ab[2 + d:3 + d, :])
        cum = (same & ((jr >= ic) if d else (jr <= ic))).astype(f32)
        rows = jnp.concatenate([g, jnp.zeros((7, seg), f32)], axis=0)
        gc = jnp.dot(rows, cum, precision=HI, preferred_element_type=f32)[0:1, :]
        cols = []
        for p in range(seg // LANES):
            sl = slice(p * LANES, (p + 1) * LANES)
            cols.append(jnp.concatenate([gc[:, sl], beta[:, sl], jnp.zeros((LANES - 2, LANES), f32)], axis=0).T)
        cols = jnp.concatenate(cols, axis=0)
        return gc, cols[:, 0:1], cols[:, 1:2]

    inst = []
    for d, (q_ref, k_ref, v_ref, ab_ref) in enumerate(((qf, kf, vf, abf), (qb, kb, vb, abb))):
        gcr, gcc, bcol = prep(ab_ref, d)
        for ci in range(nc):
            sl = slice(ci * CHUNK, (ci + 1) * CHUNK)
            gl = gcr[:, ci * CHUNK:ci * CHUNK + 1] if d else gcr[:, (ci + 1) * CHUNK - 1:(ci + 1) * CHUNK]
            inst.append(dict(d=d, sl=sl, q=q_ref[0, sl, :], k=k_ref[0, sl, :], v=v_ref[0, sl, :],
                             gcc=gcc[sl], gcr=gcr[:, sl], b=bcol[sl], gl=gl))
    for t in inst:
        incl = (r <= c) if t['d'] else (r >= c)
        t['strict'] = (r < c) if t['d'] else (r > c)
        t['decay'] = jnp.where(incl, jnp.exp(jnp.where(incl, t['gcc'] - t['gcr'], 0.0)), 0.0)
        t['kb'] = t['k'] * t['b']
        t['eg'] = jnp.exp(t['gcc'])
    kk = [_dot_nt(t['kb'], t['k']) for t in inst]
    qk = [_dot_nt(t['q'], t['k']) for t in inst]
    tinv = _tri_inverse_many([jnp.where(t['strict'], a * t['decay'], 0.0) for t, a in zip(inst, kk)])
    sol = [_dot(x, jnp.concatenate([t['v'] * t['b'], t['kb'] * t['eg']], axis=1)) for t, x in zip(inst, tinv)]
    for t, s_, a in zip(inst, sol, qk):
        t['u'], t['w'] = s_[:, :HEAD_DIM], s_[:, HEAD_DIM:]
        t['qk'] = a * t['decay']
        t['qd'] = t['q'] * t['eg']
        t['kd'] = t['k'] * jnp.exp(t['gl'] - t['gcc'])
        t['tot'] = jnp.exp(t['gl'])

    st = [sf[...], sb[...]]
    fwd = [t for t in inst if t['d'] == 0]
    bwd = [t for t in inst if t['d'] == 1][::-1]
    for tf, tb in zip(fwd, bwd):
        for t, o_ref in ((tf, of_ref), (tb, ob_ref)):
            s_ = st[t['d']]
            v_new = t['u'] - _dot(t['w'], s_)
            o_ref[0, t['sl'], :] = _dot(t['qd'], s_) + _dot(t['qk'], v_new)
            st[t['d']] = t['tot'] * s_ + _dot_tn(t['kd'], v_new)
    sf[...] = st[0]
    sb[...] = st[1]

    @pl.when(step == pl.num_programs(2) - 1)
    def _():
        sf_out[0, 0] = st[0]
        sb_out[0, 0] = st[1]


def _gdn_scan(qkv, abt, a_log, dt_bias, s0f, s0b, nc):
    bsz, seq, _ = qkv.shape
    seg = nc * CHUNK
    n_steps = seq // seg
    pos = lambda s, rev: (n_steps - 1 - s) if rev else s
    in_spec = lambda cb, rev: pl.BlockSpec((1, seg, HEAD_DIM), lambda b, h, s: (b, pos(s, rev), cb + h))
    ab_spec = lambda rev: pl.BlockSpec((1, 1, 4, seg), lambda b, h, s: (b, h, 0, pos(s, rev)))
    out_spec = lambda rev: pl.BlockSpec((1, seg, HEAD_DIM), lambda b, h, s: (b, pos(s, rev), h))
    smem = pl.BlockSpec(memory_space=pltpu.SMEM)
    st_spec = pl.BlockSpec((1, 1, HEAD_DIM, HEAD_DIM), lambda b, h, s: (b, h, 0, 0))
    o_shape = jax.ShapeDtypeStruct((bsz, seq, HEADS * HEAD_DIM), f32)
    st_shape = jax.ShapeDtypeStruct((bsz, HEADS, HEAD_DIM, HEAD_DIM), f32)
    return pl.pallas_call(
        functools.partial(_gdn_kernel, nc=nc),
        grid=(bsz, HEADS, n_steps),
        in_specs=[smem, smem,
                  in_spec(0, False), in_spec(HEADS, False), in_spec(2 * HEADS, False), ab_spec(False),
                  in_spec(0, True), in_spec(HEADS, True), in_spec(2 * HEADS, True), ab_spec(True),
                  st_spec, st_spec],
        out_specs=[out_spec(False), out_spec(True), st_spec, st_spec],
        out_shape=[o_shape, o_shape, st_shape, st_shape],
        scratch_shapes=[pltpu.VMEM((HEAD_DIM, HEAD_DIM), f32), pltpu.VMEM((HEAD_DIM, HEAD_DIM), f32)],
        compiler_params=pltpu.CompilerParams(dimension_semantics=("parallel", "parallel", "arbitrary")),
        name="gdn_scan",
    )(a_log, dt_bias, qkv, qkv, qkv, abt, qkv, qkv, qkv, abt, s0f, s0b)


def _gla_chunk(q_raw, f_raw, v, lb, st, reverse):
    r, c = _chunk_iota()
    mask = (r <= c) if reverse else (r >= c)
    q = q_raw * jax.nn.sigmoid(q_raw) * (HEAD_DIM ** -0.5)
    f = lb + (1.0 - lb) * jax.nn.sigmoid(f_raw)
    logf = jnp.log(f)
    k = 1.0 - f
    bc = jnp.dot(mask.astype(f32), logf, precision=HI, preferred_element_type=f32)
    ref_i = CHUNK // 2 - 1 if reverse else CHUNK // 2
    last_i = 0 if reverse else CHUNK - 1
    ref = bc[ref_i:ref_i + 1, :]
    last = bc[last_i:last_i + 1, :]
    sc = jnp.where(mask, _dot_nt(q * jnp.exp(bc - ref), k * jnp.exp(ref - bc)), 0.0)
    o = _dot(sc, v) + _dot_nt(q * jnp.exp(bc), st)
    st = st * jnp.exp(last) + _dot_tn(v, k * jnp.exp(last - bc))
    return o, st


def _gla_kernel(qf, ff, vf, qb, fb, vb, lb_ref, s0f, s0b, of_ref, ob_ref, sf_out, sb_out, sf, sb):
    step = pl.program_id(2)

    @pl.when(step == 0)
    def _():
        sf[...] = s0f[0, 0]
        sb[...] = s0b[0, 0]

    lb = lb_ref[...]
    st_f = sf[...]
    st_b = sb[...]
    for half in range(SEG // CHUNK):
        sl = slice(half * CHUNK, (half + 1) * CHUNK)
        o, st_f = _gla_chunk(qf[0, sl, :], ff[0, sl, :], vf[0, sl, :], lb, st_f, False)
        of_ref[0, sl, :] = o
        sl = slice(SEG - (half + 1) * CHUNK, SEG - half * CHUNK)
        o, st_b = _gla_chunk(qb[0, sl, :], fb[0, sl, :], vb[0, sl, :], lb, st_b, True)
        ob_ref[0, sl, :] = o
    sf[...] = st_f
    sb[...] = st_b

    @pl.when(step == pl.num_programs(2) - 1)
    def _():
        sf_out[0, 0] = st_f
        sb_out[0, 0] = st_b


def _gla_scan(p3, col_blocks, lb, s0f, s0b, n_steps, column_major):
    bsz = p3.shape[0]
    cq, cff, cfb, cv = col_blocks
    pos = lambda s, rev: (n_steps - 1 - s) if rev else s
    if column_major:
        lane_blocks = p3.shape[2] // (n_steps * LANES)
        in_spec = lambda cb, rev: pl.BlockSpec((1, SEG, LANES), lambda b, h, s: (b, 0, pos(s, rev) * lane_blocks + cb + h))
        out_spec = lambda rev: pl.BlockSpec((1, SEG, LANES), lambda b, h, s: (b, 0, pos(s, rev) * HEADS + h))
        o_shape = jax.ShapeDtypeStruct((bsz, SEG, n_steps * HEADS * HEAD_DIM), f32)
    else:
        in_spec = lambda cb, rev: pl.BlockSpec((1, SEG, LANES), lambda b, h, s: (b, pos(s, rev), cb + h))
        out_spec = lambda rev: pl.BlockSpec((1, SEG, LANES), lambda b, h, s: (b, pos(s, rev), h))
        o_shape = jax.ShapeDtypeStruct((bsz, n_steps * SEG, HEADS * HEAD_DIM), f32)
    st_spec = pl.BlockSpec((1, 1, HEAD_DIM, HEAD_DIM), lambda b, h, s: (b, h, 0, 0))
    st_shape = jax.ShapeDtypeStruct((bsz, HEADS, HEAD_DIM, HEAD_DIM), f32)
    return pl.pallas_call(
        _gla_kernel,
        grid=(bsz, HEADS, n_steps),
        in_specs=[in_spec(cq, False), in_spec(cff, False), in_spec(cv, False),
                  in_spec(cq, True), in_spec(cfb, True), in_spec(cv, True),
                  pl.BlockSpec((1, HEAD_DIM), lambda b, h, s: (0, h)), st_spec, st_spec],
        out_specs=[out_spec(False), out_spec(True), st_spec, st_spec],
        out_shape=[o_shape, o_shape, st_shape, st_shape],
        scratch_shapes=[pltpu.VMEM((HEAD_DIM, HEAD_DIM), f32), pltpu.VMEM((HEAD_DIM, HEAD_DIM), f32)],
        compiler_params=pltpu.CompilerParams(dimension_semantics=("parallel", "parallel", "arbitrary")),
        name="hgrn2_scan",
    )(p3, p3, p3, p3, p3, p3, lb.reshape(1, -1), s0f, s0b)


_W = HEADS * HEAD_DIM
C_QKV, C_GA, C_QB, C_FF, C_FB, C_IB, C_GB, C_MG, C_AB = (0, 3 * _W, 4 * _W, 5 * _W, 6 * _W, 7 * _W, 8 * _W, 9 * _W, 11 * _W)
N_MAIN = 11 * _W
PROJ_TM = 512
MERGE_TM = 256


def _modulated_norm(x, w, shift, scale):
    return x * lax.rsqrt(jnp.mean(x * x, axis=-1, keepdims=True) + EPS) * w * (1.0 + scale) + shift


def _in_proj_kernel(x_ref, nw_ref, sh_ref, sc_ref, w_ref, wab_ref, p_ref, ab_ref, h_scr):
    j = pl.program_id(1)

    @pl.when(j == 0)
    def _():
        h = _modulated_norm(x_ref[...], nw_ref[...], sh_ref[0], sc_ref[0]).astype(bf16)
        h_scr[...] = h
        ab_ref[...] = jnp.dot(h, wab_ref[...], preferred_element_type=f32)

    p_ref[...] = jnp.dot(h_scr[...], w_ref[...], preferred_element_type=f32)


def _in_proj(x2, norm_w, shift, scale, w_main, w_ab):
    rows, d = x2.shape
    tm = min(PROJ_TM, rows // shift.shape[0])
    per_mod = rows // shift.shape[0] // tm
    mod_spec = pl.BlockSpec((1, 1, d), lambda i, j: (i // per_mod, 0, 0))
    return pl.pallas_call(
        _in_proj_kernel,
        grid=(rows // tm, N_MAIN // _W),
        in_specs=[pl.BlockSpec((tm, d), lambda i, j: (i, 0)),
                  pl.BlockSpec((1, d), lambda i, j: (0, 0)), mod_spec, mod_spec,
                  pl.BlockSpec((d, _W), lambda i, j: (0, j)),
                  pl.BlockSpec((d, LANES), lambda i, j: (0, 0))],
        out_specs=[pl.BlockSpec((tm, _W), lambda i, j: (i, j)), pl.BlockSpec((tm, LANES), lambda i, j: (i, 0))],
        out_shape=[jax.ShapeDtypeStruct((rows, N_MAIN), f32), jax.ShapeDtypeStruct((rows, LANES), f32)],
        scratch_shapes=[pltpu.VMEM((tm, d), bf16)],
        compiler_params=pltpu.CompilerParams(dimension_semantics=("parallel", "arbitrary")),
        name="in_proj",
    )(x2, norm_w.reshape(1, d), shift[:, None, :], scale[:, None, :], w_main, w_ab)


def _split_w_in(w_in):
    ab0 = 4 * _W
    ab1 = ab0 + 4 * HEADS
    pad = jnp.zeros((w_in.shape[0], LANES - 4 * HEADS), w_in.dtype)
    w_main = jnp.concatenate([w_in[:, :ab0], w_in[:, ab1:]], axis=1)
    w_ab = jnp.concatenate([w_in[:, ab0:ab1], pad], axis=1)
    return w_main.astype(bf16), w_ab.astype(bf16)


def _mixer_scans(p, ab, p_ctx, ab_ctx, conv_w, a_log, dt_bias, lb):
    bsz, seq, _ = p.shape
    ctx_len = p_ctx.shape[1]
    rows = seq // GRID_W
    zeros = jnp.zeros((bsz, HEADS, HEAD_DIM, HEAD_DIM), f32)
    abt = lambda t: t[:, :, :4 * HEADS].reshape(bsz, -1, 4, HEADS).transpose(0, 3, 2, 1)
    _, _, sa_f, sa_b = _gdn_scan(_qkv_conv(p_ctx, conv_w, ctx_len), abt(ab_ctx), a_log, dt_bias, zeros, zeros,
                                 min(GDN_NC, ctx_len // CHUNK))
    oa_f, oa_b, _, _ = _gdn_scan(_qkv_conv(p, conv_w, 512), abt(ab), a_log, dt_bias, sa_f, sa_b, GDN_NC)
    cols = tuple(c // LANES for c in (C_QB, C_FF, C_FB, C_IB))
    _, _, sb_f, sb_b = _gla_scan(p_ctx, cols, lb, zeros, zeros, ctx_len // SEG, False)
    pb = p[:, :, C_QB:C_GB].reshape(bsz, rows, GRID_W * (C_GB - C_QB))
    cols_b = tuple((c - C_QB) // LANES for c in (C_QB, C_FF, C_FB, C_IB))
    ob_f, ob_b, _, _ = _gla_scan(pb, cols_b, lb, sb_f, sb_b, GRID_W, True)
    flat = lambda t: t.reshape(bsz * seq, _W)
    return flat(oa_f), flat(oa_b), flat(ob_f), flat(ob_b)


def _merge_kernel(oaf, oab, obf, obb, ga, gb, mga, mgb, x_ref, g1_ref, sh2_ref, sc2_ref, na_ref, nb_ref, n2_ref,
                  wpa, wpb, wo, x1_ref, t_ref):
    def head_out(o, nw_ref, gate):
        outs = []
        for h in range(HEADS):
            seg = o[:, h * HEAD_DIM:(h + 1) * HEAD_DIM]
            outs.append(seg * lax.rsqrt(jnp.mean(seg * seg, axis=1, keepdims=True) + EPS) * nw_ref[...])
        return jnp.concatenate(outs, axis=1) * (gate * jax.nn.sigmoid(gate))

    o_a = head_out(oaf[...] + oab[...], na_ref, ga[...])
    o_b = head_out(obf[...] + obb[...], nb_ref, gb[...])
    merged = jax.nn.sigmoid(mga[...]) * _dot(o_a, wpa[...]) + jax.nn.sigmoid(mgb[...]) * _dot(o_b, wpb[...])
    x1 = x_ref[...] + g1_ref[0] * _dot(merged, wo[...])
    x1_ref[...] = x1
    t_ref[...] = _modulated_norm(x1, n2_ref[...], sh2_ref[0], sc2_ref[0])


def _merge(scans, p2, x2, g1, sh2, sc2, gdn_norm_w, hg_norm_w, norm2_w, w_pa, w_pb, w_o):
    n_tok, d = x2.shape
    tm = MERGE_TM
    per_mod = n_tok // g1.shape[0] // tm
    row = lambda cb: pl.BlockSpec((tm, d), lambda i: (i, cb))
    mod = pl.BlockSpec((1, 1, d), lambda i: (i // per_mod, 0, 0))
    vec = lambda n: pl.BlockSpec((1, n), lambda i: (0, 0))
    mat = pl.BlockSpec((d, d), lambda i: (0, 0))
    return pl.pallas_call(
        _merge_kernel,
        grid=(n_tok // tm,),
        in_specs=[row(0), row(0), row(0), row(0),
                  row(C_GA // d), row(C_GB // d), row(C_MG // d), row(C_MG // d + 1),
                  row(0), mod, mod, mod, vec(HEAD_DIM), vec(HEAD_DIM), vec(d), mat, mat, mat],
        out_specs=[row(0), row(0)],
        out_shape=[jax.ShapeDtypeStruct((n_tok, d), f32), jax.ShapeDtypeStruct((n_tok, d), f32)],
        compiler_params=pltpu.CompilerParams(dimension_semantics=("parallel",), vmem_limit_bytes=48 * 1024 * 1024),
        name="merge",
    )(*scans, p2, p2, p2, p2, x2, g1[:, None, :], sh2[:, None, :], sc2[:, None, :], gdn_norm_w.reshape(1, -1), hg_norm_w.reshape(1, -1),
      norm2_w.reshape(1, -1), w_pa.astype(bf16), w_pb.astype(bf16), w_o.astype(bf16))


def _tree(p, roll, where, sub):
    m = [where(sub < 4, a + roll(a, 4), b + roll(b, 4)) for a, b in zip(p[0::2], p[1::2])]
    q = [where((sub & 2) != 0, a + roll(a, 2), b + roll(b, 6)) for a, b in zip(m[0::2], m[1::2])]
    return where((sub & 1) != 0, q[0] + roll(q[0], 1), q[1] + roll(q[1], 7))


def _tree_order():
    sub = np.arange(SUBLANES)[:, None]
    out = _tree([np.full((SUBLANES, 1), 10.0 ** i) for i in range(SUBLANES)],
                lambda a, s: np.roll(a, s, axis=0), np.where, sub)
    src = [int(round(np.log10(v / SUBLANES))) for v in out[:, 0]]
    assert sorted(src) == list(range(SUBLANES))
    order = [0] * SUBLANES
    for r, i in enumerate(src):
        order[i] = r
    return tuple(order)


_TREE_ORDER = _tree_order()


def _peer_expert_kernel(idx_ref, t_ref, g_ref, uv_hbm, o_ref, buf, sems, a_scr, ab_scr):
    i = pl.program_id(0)
    n = pl.num_programs(0) - 1

    def issue_token(j, slot):
        for k in range(PEER_K):
            pltpu.make_async_copy(uv_hbm.at[idx_ref[j, k]], buf.at[slot, j, k], sems.at[slot]).start()

    def wait_slot(slot):
        for j in range(PEER_TB):
            pltpu.make_async_copy(uv_hbm.at[pl.ds(0, PEER_K)], buf.at[slot, j], sems.at[slot]).wait()

    @pl.when(i == 0)
    def _():
        def tok(j, carry):
            issue_token(j, 0)
            return carry
        lax.fori_loop(0, PEER_TB, tok, 0)

    @pl.when(i > 0)
    def _():
        slot = (i - 1) % 2
        wait_slot(slot)
        sub = lax.broadcasted_iota(jnp.int32, (SUBLANES, LANES), 0)
        lane = lax.broadcasted_iota(jnp.int32, (PEER_K, LANES), 1)
        roll = lambda a, s: pltpu.roll(a, s, axis=0)

        def tok(j, s_all):
            issue_token(j, i % 2)
            t = t_ref[j]
            cols = []
            for g in range(PEER_K // SUBLANES):
                p = [buf[slot, j, g * SUBLANES + r, 0:SUBLANES, :] * t for r in _TREE_ORDER]
                cols.append(jnp.sum(_tree(p, roll, jnp.where, sub), axis=1, keepdims=True))
            return jnp.where(lane == j, jnp.concatenate(cols, axis=0), s_all)

        s = lax.fori_loop(0, PEER_TB, tok, jnp.zeros((PEER_K, LANES), f32))[:, 0:PEER_TB]
        a_scr[...] = 0.5 * s * (1.0 + lax.erf(s * (2.0 ** -0.5))) * g_ref[0]
        for j in range(PEER_TB):
            ab_scr[...] = jnp.broadcast_to(a_scr[:, j:j + 1], (PEER_K, LANES))
            acc = jnp.zeros((SUBLANES, LANES), f32)
            for k in range(PEER_K):
                acc = acc + ab_scr[k:k + 1, :] * buf[slot, j, k, SUBLANES:2 * SUBLANES, :]
            o_ref[j] = acc

    @pl.when(i == n)
    def _():
        wait_slot(n % 2)


def _peer_experts(t, experts, gates_t, expert_u, expert_v):
    n_tok, d = t.shape
    assert d == SUBLANES * LANES, "one expert row is one (8, 128) tile"
    nb = n_tok // PEER_TB
    uv = jnp.concatenate([expert_u, expert_v], axis=1).reshape(-1, 2 * SUBLANES, LANES)
    prev = lambda i: jnp.maximum(i - 1, 0)
    out = pl.pallas_call(
        _peer_expert_kernel,
        grid=(nb + 1,),
        in_specs=[pl.BlockSpec((PEER_TB, PEER_K), lambda i: (jnp.minimum(i, nb - 1), 0), memory_space=pltpu.SMEM),
                  pl.BlockSpec((PEER_TB, SUBLANES, LANES), lambda i: (prev(i), 0, 0)),
                  pl.BlockSpec((1, PEER_K, PEER_TB), lambda i: (prev(i), 0, 0)),
                  pl.BlockSpec(memory_space=pl.ANY)],
        out_specs=pl.BlockSpec((PEER_TB, SUBLANES, LANES), lambda i: (prev(i), 0, 0)),
        out_shape=jax.ShapeDtypeStruct((n_tok, SUBLANES, LANES), f32),
        scratch_shapes=[pltpu.VMEM((2, PEER_TB, PEER_K, 2 * SUBLANES, LANES), f32),
                        pltpu.SemaphoreType.DMA((2,)),
                        pltpu.VMEM((PEER_K, PEER_TB), f32),
                        pltpu.VMEM((PEER_K, LANES), f32)],
        compiler_params=pltpu.CompilerParams(dimension_semantics=("arbitrary",),
                                             vmem_limit_bytes=40 * 1024 * 1024),
        name="peer_experts",
    )(experts, t.reshape(n_tok, SUBLANES, LANES), gates_t, uv)
    return out.reshape(n_tok, d)


ROUTE_TM = 256
NEG = -jnp.inf
BIG = 2 ** 30


def _top_rows(x, ids, val_ref, id_ref):
    for it in range(P_TOPK):
        m = jnp.max(x, axis=0, keepdims=True)
        sel = jnp.min(jnp.where(x == m, ids, BIG), axis=0, keepdims=True)
        val_ref[it:it + 1, :] = m
        id_ref[it:it + 1, :] = sel
        x = jnp.where(ids == sel, NEG, x)


def _route_kernel(t_ref, wq_ref, keys_ref, exp_ref, gate_ref, v1, i1, v2, i2, vt, it_):
    tm = t_ref.shape[0]
    q = _dot(t_ref[...], wq_ref[...])
    key_ids = lax.broadcasted_iota(jnp.int32, (P_NKEYS, tm), 0)
    sub = lax.broadcasted_iota(jnp.int32, (8, tm), 0)
    for h in range(P_HEADS):
        for c, (vr, ir) in enumerate(((v1, i1), (v2, i2))):
            qb = q[:, (2 * h + c) * P_DQ:(2 * h + c + 1) * P_DQ]
            _top_rows(_dot_nt(keys_ref[h, c], qb), key_ids, vr, ir)
        s1, s2 = v1[...], v2[...]
        cands, cids = [s1[0:1, :] + s2], [lax.broadcasted_iota(jnp.int32, (P_TOPK, tm), 0)]
        for a in range(1, P_TOPK):
            nb = P_TOPK // (a + 1)
            cands.append(jnp.where(sub < nb, s1[a:a + 1, :] + s2[0:8, :], NEG))
            cids.append(sub + a * P_TOPK)
        _top_rows(jnp.concatenate(cands, axis=0), jnp.concatenate(cids, axis=0), vt, it_)
        top_s, top_i = vt[...], it_[...]
        ra = jnp.right_shift(top_i, 4)
        rb = jnp.bitwise_and(top_i, P_TOPK - 1)
        e1 = jnp.zeros_like(top_i)
        e2 = jnp.zeros_like(top_i)
        k1, k2 = i1[...], i2[...]
        for a in range(P_TOPK):
            e1 = jnp.where(ra == a, k1[a:a + 1, :], e1)
            e2 = jnp.where(rb == a, k2[a:a + 1, :], e2)
        exp_ref[h * P_TOPK:(h + 1) * P_TOPK, :] = e1 * P_NKEYS + e2
        e = jnp.exp(top_s - top_s[0:1, :])
        gate_ref[h * P_TOPK:(h + 1) * P_TOPK, :] = e / jnp.sum(e, axis=0, keepdims=True)


def _peer_route(t, w_query, sub_keys):
    n_tok, d = t.shape
    tm = ROUTE_TM
    scr = lambda dt: pltpu.VMEM((P_TOPK, tm), dt)
    return pl.pallas_call(
        _route_kernel,
        grid=(n_tok // tm,),
        in_specs=[pl.BlockSpec((tm, d), lambda i: (i, 0)),
                  pl.BlockSpec(w_query.shape, lambda i: (0, 0)),
                  pl.BlockSpec(sub_keys.shape, lambda i: (0, 0, 0, 0))],
        out_specs=[pl.BlockSpec((PEER_K, tm), lambda i: (0, i)), pl.BlockSpec((PEER_K, tm), lambda i: (0, i))],
        out_shape=[jax.ShapeDtypeStruct((PEER_K, n_tok), jnp.int32), jax.ShapeDtypeStruct((PEER_K, n_tok), f32)],
        scratch_shapes=[scr(f32), scr(jnp.int32), scr(f32), scr(jnp.int32), scr(f32), scr(jnp.int32)],
        compiler_params=pltpu.CompilerParams(dimension_semantics=("parallel",)),
        name="peer_route",
    )(t, w_query.astype(bf16), sub_keys.astype(bf16))


def _peer(t, w_query, sub_keys, expert_u, expert_v):
    n_tok = t.shape[0]
    experts, gates = _peer_route(t, w_query, sub_keys)
    gates_t = gates.reshape(PEER_K, n_tok // PEER_TB, PEER_TB).transpose(1, 0, 2)
    return _peer_experts(t, experts.T, gates_t, expert_u, expert_v)


def _final_norm_kernel(x_ref, y_ref, g_ref, w_ref, o_ref):
    x = x_ref[...] + g_ref[0] * y_ref[...]
    o_ref[...] = x * lax.rsqrt(jnp.mean(x * x, axis=-1, keepdims=True) + EPS) * w_ref[...]


def _final_norm(x1, y, g2, w):
    n, d = x1.shape
    tm = PROJ_TM
    per_mod = n // g2.shape[0] // tm
    row = pl.BlockSpec((tm, d), lambda i: (i, 0))
    return pl.pallas_call(
        _final_norm_kernel,
        grid=(n // tm,),
        in_specs=[row, row, pl.BlockSpec((1, 1, d), lambda i: (i // per_mod, 0, 0)),
                  pl.BlockSpec((1, d), lambda i: (0, 0))],
        out_specs=row,
        out_shape=jax.ShapeDtypeStruct((n, d), f32),
        compiler_params=pltpu.CompilerParams(dimension_semantics=("parallel",)),
        name="final_norm",
    )(x1, y, g2[:, None, :], w.reshape(1, d))


def kernel(x, c, ctx, c_ctx, w_ada, b_ada, norm1_w, w_in, conv_w, a_log, dt_bias, gdn_norm_w, lb_logits,
           hg_norm_w, w_pa, w_pb, w_o, norm2_w, w_query, sub_keys, expert_u, expert_v, final_norm_w):
    assert w_ada.shape[0] == 1, "single-layer block"
    bsz, seq, d = x.shape
    assert seq // GRID_W == SEG, "the column-major scan reads one grid column (SEG rows) per step"
    l = 0
    lb = jnp.cumsum(jax.nn.softmax(lb_logits.astype(f32), axis=0), axis=0)[l]
    mod = jax.nn.silu(c) @ w_ada[l] + b_ada[l]
    mod_c = (jax.nn.silu(c_ctx) @ w_ada[l] + b_ada[l])[None, :]
    sh1, sc1, g1, sh2, sc2, g2 = jnp.split(mod, 6, axis=-1)
    sh1c, sc1c = mod_c[:, :d], mod_c[:, d:2 * d]
    w_main, w_ab = _split_w_in(w_in[l])
    x2 = x.reshape(bsz * seq, d)
    p, ab = _in_proj(x2, norm1_w[l], sh1, sc1, w_main, w_ab)
    p_ctx, ab_ctx = _in_proj(ctx.reshape(-1, d), norm1_w[l], sh1c, sc1c, w_main, w_ab)
    three_d = lambda t, n: t.reshape(bsz, -1, n)
    scans = _mixer_scans(three_d(p, N_MAIN), three_d(ab, LANES), three_d(p_ctx, N_MAIN), three_d(ab_ctx, LANES),
                         conv_w[l], a_log[l], dt_bias[l], lb)
    x1, t = _merge(scans, p, x2, g1, sh2, sc2, gdn_norm_w[l], hg_norm_w[l], norm2_w[l], w_pa[l], w_pb[l], w_o[l])
    y = _peer(t, w_query[l], sub_keys[l], expert_u[l], expert_v[l])
    return _final_norm(x1, y, g2, final_norm_w).reshape(bsz, seq, d)
```
